```python
import math
import jax, jax.numpy as jnp
from jax import lax
import numpy as np

D_MODEL = 1024
BATCH = 4
SEQ = 8192
DEPTH = 4
DEC_BATCH = 16
DEC_SEQ = 64
PAST_LEN = 4096

CHUNK = 64
Q_BLOCK = 128
NORM_EPS = 1e-6
NEG_INF = -1e30

MLA_HEADS = 8
MLA_NOPE_DIM = 64
MLA_ROPE_DIM = 32
MLA_V_DIM = 64
MLA_Q_RANK = 384
MLA_KV_RANK = 256
ROPE_BASE = 10000.0

FOX_HEADS = 8
FOX_HEAD_DIM = 64
FOX_BIAS_INIT = 4.0

DIFF_HEADS = 4
DIFF_QK_DIM = 64
DIFF_V_DIM = 2 * DIFF_QK_DIM

MIX_WIDTH = MLA_HEADS * MLA_V_DIM
N_BRANCHES = 3

REL_BUCKETS = 32
REL_MAX_DIST = 128

D_FF = ((8 * D_MODEL + 3 * 256 - 1) // (3 * 256)) * 256

IN_SPLIT_SIZES = (
    MLA_Q_RANK, MLA_KV_RANK, MLA_ROPE_DIM,
    FOX_HEADS * FOX_HEAD_DIM, FOX_HEADS * FOX_HEAD_DIM, FOX_HEADS * FOX_HEAD_DIM, FOX_HEADS,
    DIFF_HEADS * 2 * DIFF_QK_DIM, DIFF_HEADS * 2 * DIFF_QK_DIM, DIFF_HEADS * DIFF_V_DIM,
    N_BRANCHES * D_MODEL,
)
W_IN_COLS = sum(IN_SPLIT_SIZES)

STATE_NAMES = ("mla_latent", "mla_krope", "fox_k", "fox_v", "fox_logf", "diff_k", "diff_v")

kernel_name = "gated_hybrid_mla_fox_diff_stream_step"


def rmsnorm(x, g):
    x32 = x.astype(jnp.float32)
    y = x32 * lax.rsqrt(jnp.mean(x32 * x32, axis=-1, keepdims=True) + NORM_EPS)
    return (y * g.astype(jnp.float32)).astype(x.dtype)


def rope(x, pos):
    half = x.shape[-1] // 2
    inv = 1.0 / (ROPE_BASE ** (jnp.arange(half, dtype=jnp.float32) / half))
    ang = pos.astype(jnp.float32)[:, None] * inv[None, :]
    ang = ang.reshape((1, pos.shape[0]) + (1,) * (x.ndim - 3) + (half,))
    cos, sin = jnp.cos(ang), jnp.sin(ang)
    x32 = x.astype(jnp.float32)
    x1, x2 = x32[..., :half], x32[..., half:]
    return jnp.concatenate([x1 * cos - x2 * sin, x1 * sin + x2 * cos], axis=-1).astype(x.dtype)


def chunk_causal_mask(q_pos, k_pos):
    return (k_pos[None, :] // CHUNK) <= (q_pos[:, None] // CHUNK)


def frame_causal_mask(q_pos, k_pos):
    return k_pos[None, :] <= q_pos[:, None]


def masked_softmax(s, mask):
    return jax.nn.softmax(jnp.where(mask, s, NEG_INF), axis=-1)


def rel_bucket(rel):
    half = REL_BUCKETS // 2
    max_exact = half // 2
    ret = jnp.where(rel > 0, half, 0)
    n = jnp.abs(rel)
    large = max_exact + (jnp.log(jnp.maximum(n, 1).astype(jnp.float32) / max_exact)
                         / math.log(REL_MAX_DIST / max_exact) * (half - max_exact)).astype(jnp.int32)
    large = jnp.minimum(large, half - 1)
    return ret + jnp.where(n < max_exact, n, large)


def concat_past(past, new):
    return new if past is None else jnp.concatenate([past, new.astype(past.dtype)], axis=1)


def sweep_queries(block_fn, q_arrays, q_pos):
    n = q_pos.shape[0]
    if n <= Q_BLOCK or n % Q_BLOCK != 0:
        return block_fn(*q_arrays, q_pos)
    nb = n // Q_BLOCK

    def split(a):
        return jnp.moveaxis(a.reshape((a.shape[0], nb, Q_BLOCK) + a.shape[2:]), 1, 0)

    xs = tuple(split(a) for a in q_arrays) + (q_pos.reshape(nb, Q_BLOCK),)
    out = lax.map(lambda args: block_fn(*args), xs)
    out = jnp.moveaxis(out, 0, 1)
    return out.reshape((out.shape[0], n) + out.shape[3:])


def mla_mixer(cq, ckv, kr, q_pos, k_pos, past_latent, past_krope, g_q, g_kv, w_uq, w_ukv):
    B, n, _ = cq.shape
    q = (rmsnorm(cq, g_q) @ w_uq).reshape(B, n, MLA_HEADS, MLA_NOPE_DIM + MLA_ROPE_DIM)
    q_nope = q[..., :MLA_NOPE_DIM]
    q_rope = rope(q[..., MLA_NOPE_DIM:], q_pos)
    latent = rmsnorm(ckv, g_kv)
    krope = rope(kr, q_pos)
    all_lat = concat_past(past_latent, latent)
    all_kr = concat_past(past_krope, krope)
    Lk = all_lat.shape[1]
    kv = (all_lat @ w_ukv).reshape(B, Lk, MLA_HEADS, MLA_NOPE_DIM + MLA_V_DIM)
    k_nope, v = kv[..., :MLA_NOPE_DIM], kv[..., MLA_NOPE_DIM:]
    scale = (MLA_NOPE_DIM + MLA_ROPE_DIM) ** -0.5

    def block(qn, qr, qp):
        s = (jnp.einsum("bqhd,bkhd->bhqk", qn, k_nope)
             + jnp.einsum("bqhr,bkr->bhqk", qr, all_kr)).astype(jnp.float32) * scale
        p = masked_softmax(s, chunk_causal_mask(qp, k_pos)).astype(v.dtype)
        return jnp.einsum("bhqk,bkhd->bqhd", p, v)

    o = sweep_queries(block, (q_nope, q_rope), q_pos)
    return o.reshape(B, n, MLA_HEADS * MLA_V_DIM), latent, krope


def fox_mixer(q, k, v, f_logit, f_bias, q_pos, k_pos, past_k, past_v, past_logf):
    B, n = q.shape[:2]
    logf = jax.nn.log_sigmoid(f_logit.astype(jnp.float32) + f_bias.astype(jnp.float32)).astype(q.dtype)
    all_k = concat_past(past_k, k)
    all_v = concat_past(past_v, v)
    all_logf = concat_past(past_logf, logf)
    Lk = all_k.shape[1]
    c = jnp.cumsum(all_logf.astype(jnp.float32), axis=1)
    c_k = jnp.transpose(c, (0, 2, 1))
    c_q = c[:, Lk - n:]
    scale = FOX_HEAD_DIM ** -0.5

    def block(qb, cqb, qp):
        s = jnp.einsum("bqhd,bkhd->bhqk", qb, all_k).astype(jnp.float32) * scale
        s = s + jnp.transpose(cqb, (0, 2, 1))[..., None] - c_k[:, :, None, :]
        p = masked_softmax(s, frame_causal_mask(qp, k_pos)).astype(all_v.dtype)
        return jnp.einsum("bhqk,bkhd->bqhd", p, all_v)

    o = sweep_queries(block, (q, c_q), q_pos)
    return o.reshape(B, n, FOX_HEADS * FOX_HEAD_DIM), k, v, logf


def diff_mixer(q, k, v, q_pos, k_pos, past_k, past_v, rel_table, lam, lam_init, g_sub):
    B, n = q.shape[:2]
    all_k = concat_past(past_k, k)
    all_v = concat_past(past_v, v)
    Lk = all_k.shape[1]
    all_k2 = all_k.reshape(B, Lk, DIFF_HEADS, 2, DIFF_QK_DIM)
    scale = DIFF_QK_DIM ** -0.5

    def block(qb, qp):
        bias = rel_table[rel_bucket(k_pos[None, :] - qp[:, None])]
        bias = jnp.transpose(bias, (2, 0, 1)).astype(jnp.float32)
        s = jnp.einsum("bqhcd,bkhcd->bchqk", qb, all_k2).astype(jnp.float32) * scale + bias
        p = masked_softmax(s, chunk_causal_mask(qp, k_pos))
        pd = (p[:, 0] - lam * p[:, 1]).astype(all_v.dtype)
        return jnp.einsum("bhqk,bkhd->bqhd", pd, all_v)

    o = sweep_queries(block, (q,), q_pos)
    o = rmsnorm(o, g_sub) * (1.0 - lam_init)
    return o.reshape(B, n, DIFF_HEADS * DIFF_V_DIM), k, v


def trunk_layer(x, q_pos, k_pos, past, li, p, rel_table):
    B, n, _ = x.shape
    h = rmsnorm(x, p["attn_norm_g"])
    points = np.cumsum(IN_SPLIT_SIZES)[:-1].tolist()
    (cq, ckv, kr, fq, fk, fv, ff, dq, dk, dv, gl) = jnp.split(h @ p["w_in"], points, axis=-1)

    o_a, lat, krope = mla_mixer(cq, ckv, kr, q_pos, k_pos, past.get("mla_latent"), past.get("mla_krope"),
                                p["mla_q_norm_g"], p["mla_kv_norm_g"], p["w_mla_uq"], p["w_mla_ukv"])

    o_b, fk_new, fv_new, logf = fox_mixer(
        fq.reshape(B, n, FOX_HEADS, FOX_HEAD_DIM), fk.reshape(B, n, FOX_HEADS, FOX_HEAD_DIM),
        fv.reshape(B, n, FOX_HEADS, FOX_HEAD_DIM), ff, p["fox_f_bias"], q_pos, k_pos,
        past.get("fox_k"), past.get("fox_v"), past.get("fox_logf"))

    lam_init = 0.8 - 0.6 * math.exp(-0.3 * li)
    lp = p["diff_lambda"].astype(jnp.float32)
    lam = jnp.exp(jnp.sum(lp[0] * lp[1])) - jnp.exp(jnp.sum(lp[2] * lp[3])) + lam_init
    o_c, dk_new, dv_new = diff_mixer(
        dq.reshape(B, n, DIFF_HEADS, 2, DIFF_QK_DIM), dk.reshape(B, n, DIFF_HEADS, 2 * DIFF_QK_DIM),
        dv.reshape(B, n, DIFF_HEADS, DIFF_V_DIM), q_pos, k_pos, past.get("diff_k"), past.get("diff_v"),
        rel_table, lam, lam_init, p["diff_subln_g"])

    o = jnp.stack([o_a, o_b, o_c], axis=2)
    branch = jnp.einsum("bnjw,jwd->bnjd", o, p["w_branch"])
    gates = jax.nn.sigmoid(gl.reshape(B, n, N_BRANCHES, D_MODEL).astype(jnp.float32)).astype(x.dtype)
    x = x + jnp.sum(gates * branch, axis=2) @ p["w_o"]

    h = rmsnorm(x, p["ffn_norm_g"])
    g, u = jnp.split(h @ p["w_ffn_in"], 2, axis=-1)
    x = x + (jax.nn.silu(g) * u) @ p["w_ffn_out"]

    rows = {"mla_latent": lat, "mla_krope": krope, "fox_k": fk_new, "fox_v": fv_new,
            "fox_logf": logf, "diff_k": dk_new, "diff_v": dv_new}
    return x, rows


def run_trunk(x, past, params, rel_table, final_g):
    n = x.shape[1]
    n_past = past["mla_latent"].shape[2] if past else 0
    q_pos = n_past + jnp.arange(n, dtype=jnp.int32)
    k_pos = jnp.arange(n_past + n, dtype=jnp.int32)
    new = {name: [] for name in STATE_NAMES}
    for li in range(DEPTH):
        p = {name: arr[li] for name, arr in params.items()}
        layer_past = {name: arr[li] for name, arr in past.items()}
        x, rows = trunk_layer(x, q_pos, k_pos, layer_past, li, p, rel_table)
        for name in STATE_NAMES:
            new[name].append(rows[name])
    y = rmsnorm(x, final_g)
    return y, {name: jnp.stack(new[name]) for name in STATE_NAMES}


def setup_inputs(seed: int = 0) -> dict:
    key = jax.random.key(seed)
    ks = jax.random.split(key, 26)
    f32 = jnp.float32
    nrm = jax.random.normal

    def w(k, shape, fan_in):
        return nrm(k, shape, f32) * fan_in ** -0.5

    def gain(k, shape):
        return 1.0 + 0.02 * nrm(k, shape, f32)

    cache = (DEPTH, DEC_BATCH, PAST_LEN)
    return {
        "x_prompt": nrm(ks[0], (BATCH, SEQ, D_MODEL), f32),
        "x_sample": nrm(ks[1], (DEC_BATCH, DEC_SEQ, D_MODEL), f32),
        "cache_mla_latent": nrm(ks[2], cache + (MLA_KV_RANK,), f32),
        "cache_mla_krope": nrm(ks[3], cache + (MLA_ROPE_DIM,), f32),
        "cache_fox_k": nrm(ks[4], cache + (FOX_HEADS, FOX_HEAD_DIM), f32),
        "cache_fox_v": nrm(ks[5], cache + (FOX_HEADS, FOX_HEAD_DIM), f32),
        "cache_fox_logf": jax.nn.log_sigmoid(FOX_BIAS_INIT + nrm(ks[6], cache + (FOX_HEADS,), f32)),
        "cache_diff_k": nrm(ks[7], cache + (DIFF_HEADS, 2 * DIFF_QK_DIM), f32),
        "cache_diff_v": nrm(ks[8], cache + (DIFF_HEADS, DIFF_V_DIM), f32),
        "attn_norm_g": gain(ks[9], (DEPTH, D_MODEL)),
        "w_in": w(ks[10], (DEPTH, D_MODEL, W_IN_COLS), D_MODEL),
        "mla_q_norm_g": gain(ks[11], (DEPTH, MLA_Q_RANK)),
        "mla_kv_norm_g": gain(ks[12], (DEPTH, MLA_KV_RANK)),
        "w_mla_uq": w(ks[13], (DEPTH, MLA_Q_RANK, MLA_HEADS * (MLA_NOPE_DIM + MLA_ROPE_DIM)), MLA_Q_RANK),
        "w_mla_ukv": w(ks[14], (DEPTH, MLA_KV_RANK, MLA_HEADS * (MLA_NOPE_DIM + MLA_V_DIM)), MLA_KV_RANK),
        "fox_f_bias": FOX_BIAS_INIT + 0.5 * nrm(ks[15], (DEPTH, FOX_HEADS), f32),
        "diff_lambda": 0.1 * nrm(ks[16], (DEPTH, 4, DIFF_QK_DIM), f32),
        "diff_subln_g": gain(ks[17], (DEPTH, DIFF_V_DIM)),
        "w_branch": w(ks[18], (DEPTH, N_BRANCHES, MIX_WIDTH, D_MODEL), MIX_WIDTH),
        "w_o": w(ks[19], (DEPTH, D_MODEL, D_MODEL), D_MODEL),
        "ffn_norm_g": gain(ks[20], (DEPTH, D_MODEL)),
        "w_ffn_in": w(ks[21], (DEPTH, D_MODEL, 2 * D_FF), D_MODEL),
        "w_ffn_out": w(ks[22], (DEPTH, D_FF, D_MODEL), D_FF),
        "rel_bias_table": 0.5 * nrm(ks[23], (REL_BUCKETS, DIFF_HEADS), f32),
        "final_norm_g": gain(ks[24], (D_MODEL,)),
    }


def reference(x_prompt, x_sample, cache_mla_latent, cache_mla_krope, cache_fox_k, cache_fox_v,
              cache_fox_logf, cache_diff_k, cache_diff_v, attn_norm_g, w_in, mla_q_norm_g,
              mla_kv_norm_g, w_mla_uq, w_mla_ukv, fox_f_bias, diff_lambda, diff_subln_g, w_branch,
              w_o, ffn_norm_g, w_ffn_in, w_ffn_out, rel_bias_table, final_norm_g):
    params = {
        "attn_norm_g": attn_norm_g, "w_in": w_in, "mla_q_norm_g": mla_q_norm_g,
        "mla_kv_norm_g": mla_kv_norm_g, "w_mla_uq": w_mla_uq, "w_mla_ukv": w_mla_ukv,
        "fox_f_bias": fox_f_bias, "diff_lambda": diff_lambda, "diff_subln_g": diff_subln_g,
        "w_branch": w_branch, "w_o": w_o, "ffn_norm_g": ffn_norm_g, "w_ffn_in": w_ffn_in,
        "w_ffn_out": w_ffn_out,
    }
    past = {
        "mla_latent": cache_mla_latent, "mla_krope": cache_mla_krope, "fox_k": cache_fox_k,
        "fox_v": cache_fox_v, "fox_logf": cache_fox_logf, "diff_k": cache_diff_k,
        "diff_v": cache_diff_v,
    }
    y_prompt, sp = run_trunk(x_prompt, {}, params, rel_bias_table, final_norm_g)
    y_sample, ss = run_trunk(x_sample, past, params, rel_bias_table, final_norm_g)
    return (y_prompt, y_sample,
            sp["mla_latent"], ss["mla_latent"], sp["mla_krope"], ss["mla_krope"],
            sp["fox_k"], ss["fox_k"], sp["fox_v"], ss["fox_v"],
            sp["fox_logf"], ss["fox_logf"], sp["diff_k"], ss["diff_k"],
            sp["diff_v"], ss["diff_v"])
```

```python
import functools
import math

import jax
import jax.numpy as jnp
from jax import lax
from jax.experimental import pallas as pl
from jax.experimental.pallas import tpu as pltpu

F32 = jnp.float32
BF16 = jnp.bfloat16

CHUNK_SHIFT = 6
NORM_EPS = 1e-6
NEG_INF = -1e30
MLA_HEADS, MLA_NOPE, MLA_ROPE, MLA_V = 8, 64, 32, 64
MLA_Q_RANK, MLA_KV_RANK = 384, 256
ROPE_BASE = 10000.0
FOX_HEADS, FOX_DIM = 8, 64
DIFF_HEADS, DIFF_QK, DIFF_V = 4, 64, 128
MIX_WIDTH = 512
N_BRANCHES = 3
REL_BUCKETS, REL_MAX_DIST = 32, 128

LANES = 128
VMEM_LIMIT = 56 * 1024 * 1024


def _cparams(sem):
    return pltpu.CompilerParams(dimension_semantics=sem, vmem_limit_bytes=VMEM_LIMIT)


def _rms(x, g):
    return x * lax.rsqrt(jnp.mean(x * x, axis=-1, keepdims=True) + NORM_EPS) * g


def _dot(a, b):
    return jnp.dot(a, b, preferred_element_type=F32)


def _dot_nt(a, b):
    return lax.dot_general(a, b, (((1,), (1,)), ((), ())), preferred_element_type=F32)


def _bf16_part(x):
    return x.astype(BF16).astype(F32)


def _cumsum_rows(x):
    rows = x.shape[0]
    row = lax.broadcasted_iota(jnp.int32, x.shape, 0)
    shift = 1
    while shift < rows:
        x = x + jnp.where(row >= shift, pltpu.roll(x, shift, 0), 0.0)
        shift *= 2
    return x


def _mla_in_kernel(x_ref, ga_ref, win_ref, gq_ref, wuq_ref, gkv_ref, wukv_ref,
                   cq_ref, sq_ref, ck_ref, sk_ref,
                   q_out, k_out, v_out, lat_out, kr_out):
    hb = _rms(x_ref[...], ga_ref[...]).astype(BF16)
    pm = _dot(hb, win_ref[...])
    cqn = _rms(pm[:, :MLA_Q_RANK], gq_ref[...]).astype(BF16)
    q2 = _dot(cqn, wuq_ref[...])
    hw = MLA_HEADS * LANES
    cosq = jnp.concatenate([cq_ref[...]] * MLA_HEADS, axis=1)
    sinq = jnp.concatenate([sq_ref[...]] * MLA_HEADS, axis=1)
    q_out[...] = (q2[:, :hw] * cosq + q2[:, hw:] * sinq).astype(BF16)
    latent = _rms(pm[:, MLA_Q_RANK:MLA_Q_RANK + MLA_KV_RANK], gkv_ref[...])
    lat_out[...] = latent
    kv2 = _dot(latent.astype(BF16), wukv_ref[...])
    o = MLA_Q_RANK + MLA_KV_RANK
    krb = pm[:, o:o + LANES] * ck_ref[...] + pm[:, o + LANES:o + 2 * LANES] * sk_ref[...]
    kr_out[...] = krb[:, MLA_NOPE:MLA_NOPE + MLA_ROPE]
    k_out[...] = (kv2[:, :hw] + jnp.concatenate([krb] * MLA_HEADS, axis=1)).astype(BF16)
    v_out[...] = kv2[:, hw:].astype(BF16)


def _decay_columns(c, rows):
    hi = _bf16_part(c)
    r1 = c - hi
    mid = _bf16_part(r1)
    lo = _bf16_part(r1 - mid)
    lane = lax.broadcasted_iota(jnp.int32, (rows, LANES), 1)
    d = FOX_DIM
    ones_q = jnp.where((lane >= d + 3) & (lane < d + 6), 1.0, 0.0)
    ones_k = jnp.where((lane >= d) & (lane < d + 3), 1.0, 0.0)
    eq, ek = [], []
    for h in range(FOX_HEADS):
        bh = jnp.broadcast_to(hi[:, h:h + 1], (rows, LANES))
        bm = jnp.broadcast_to(mid[:, h:h + 1], (rows, LANES))
        bl = jnp.broadcast_to(lo[:, h:h + 1], (rows, LANES))
        eq.append(jnp.where(lane == d, bh, jnp.where(lane == d + 1, bm, jnp.where(lane == d + 2, bl, ones_q))))
        ek.append(jnp.where(lane == d + 3, -bh, jnp.where(lane == d + 4, -bm, jnp.where(lane == d + 5, -bl, ones_k))))
    return jnp.concatenate(eq, axis=1), jnp.concatenate(ek, axis=1)


def _fox_in_kernel(x_ref, ga_ref, win_ref, fb_ref, sel_ref, c0_ref,
                   q_out, k_out, vb_out, kf_out, vf_out, lf_out, ct_out, carry_sc):
    rows = x_ref.shape[0]
    t = pl.program_id(1)

    @pl.when(t == 0)
    def _():
        carry_sc[...] = c0_ref[...]

    hb = _rms(x_ref[...], ga_ref[...]).astype(BF16)
    pf = _dot(hb, win_ref[...])
    w = FOX_HEADS * FOX_DIM
    fq = pf[:, :w] * (FOX_DIM ** -0.5)
    fk = pf[:, w:2 * w]
    fv = pf[:, 2 * w:3 * w]
    z = pf[:, 3 * w:3 * w + LANES] + fb_ref[...]
    lane = lax.broadcasted_iota(jnp.int32, (rows, LANES), 1)
    logf = jnp.minimum(z, 0.0) - jnp.log1p(jnp.exp(-jnp.abs(z)))
    logf = jnp.where(lane < FOX_HEADS, logf, 0.0)
    lf_out[...] = logf[:, :FOX_HEADS]
    c = carry_sc[...] + _cumsum_rows(logf)
    carry_sc[...] = c[rows - 1:rows, :]
    ct_out[...] = c[rows - 1:rows, :]
    kf_out[...] = fk
    vf_out[...] = fv
    vb_out[...] = fv.astype(BF16)
    eq, ek = _decay_columns(c, rows)
    q_out[...] = (_dot(fq.astype(BF16), sel_ref[...]) + eq).astype(BF16)
    k_out[...] = (_dot(fk.astype(BF16), sel_ref[...]) + ek).astype(BF16)


def _diff_gate_in_kernel(x_ref, ga_ref, wd_ref, wg_ref,
                         dq_out, dkb_out, dvb_out, dk_out, dv_out, gate_out):
    hb = _rms(x_ref[...], ga_ref[...]).astype(BF16)
    pd = _dot(hb, wd_ref[...])
    w = DIFF_HEADS * 2 * DIFF_QK
    dq_out[...] = (pd[:, :w] * (DIFF_QK ** -0.5)).astype(BF16)
    dk = pd[:, w:2 * w]
    dv = pd[:, 2 * w:]
    dk_out[...] = dk
    dv_out[...] = dv
    dkb_out[...] = dk.astype(BF16)
    dvb_out[...] = dv.astype(BF16)
    gate_out[...] = jax.nn.sigmoid(_dot(hb, wg_ref[...]))


def _mla_past_kernel(lat_ref, kr_ref, wukv_ref, selk_ref, k_out, v_out):
    kv2 = _dot(lat_ref[...].astype(BF16), wukv_ref[...])
    krb = _dot(kr_ref[...].astype(BF16), selk_ref[...])
    hw = MLA_HEADS * LANES
    k_out[...] = (kv2[:, :hw] + jnp.concatenate([krb] * MLA_HEADS, axis=1)).astype(BF16)
    v_out[...] = kv2[:, hw:].astype(BF16)


def _fox_past_kernel(k_ref, lf_ref, sel_ref, k_out, ct_out, carry_sc, lf_sc):
    rows = k_ref.shape[0]
    t = pl.program_id(1)

    @pl.when(t == 0)
    def _():
        carry_sc[...] = jnp.zeros(carry_sc.shape, F32)

    lf_sc[...] = jnp.zeros(lf_sc.shape, F32)
    lf_sc[:, :FOX_HEADS] = lf_ref[...]
    c = carry_sc[...] + _cumsum_rows(lf_sc[...])
    carry_sc[...] = c[rows - 1:rows, :]
    ct_out[...] = c[rows - 1:rows, :]
    _, ek = _decay_columns(c, rows)
    k_out[...] = (_dot(k_ref[...].astype(BF16), sel_ref[...]) + ek).astype(BF16)


def _block_visibility(q0, k0, tq, tk, frame):
    if frame:
        return k0 <= q0 + (tq - 1), k0 + (tk - 1) <= q0
    return ((k0 >> CHUNK_SHIFT) <= ((q0 + (tq - 1)) >> CHUNK_SHIFT),
            ((k0 + (tk - 1)) >> CHUNK_SHIFT) <= (q0 >> CHUNK_SHIFT))


def _mask(q0, k0, tq, tk, frame):
    row = q0 + lax.broadcasted_iota(jnp.int32, (tq, tk), 0)
    col = k0 + lax.broadcasted_iota(jnp.int32, (tq, tk), 1)
    if frame:
        return col <= row
    return (col >> CHUNK_SHIFT) <= (row >> CHUNK_SHIFT)


def _online_softmax_step(s, v, idx, m_sc, l_sc, acc_sc):
    dv = v.shape[1]
    m_prev = m_sc[idx]
    m_new = jnp.maximum(m_prev, jnp.max(s, axis=1, keepdims=True))
    alpha = jnp.exp(m_prev - m_new)
    p = jnp.exp(s - m_new[:, :1])
    l_sc[idx] = alpha * l_sc[idx] + jnp.sum(p, axis=1, keepdims=True)
    m_sc[idx] = m_new
    acc_sc[idx] = acc_sc[idx] * alpha[:, :dv] + _dot(p.astype(BF16), v)


def _flash_kernel(q_ref, k_ref, v_ref, o_ref, m_sc, l_sc, acc_sc, *, heads, dv, tq, tk, q_off, frame):
    qi, kj = pl.program_id(1), pl.program_id(2)
    q0 = q_off + qi * tq
    k0 = kj * tk

    @pl.when(kj == 0)
    def _():
        m_sc[...] = jnp.full(m_sc.shape, NEG_INF, F32)
        l_sc[...] = jnp.zeros(l_sc.shape, F32)
        acc_sc[...] = jnp.zeros(acc_sc.shape, F32)

    needed, full = _block_visibility(q0, k0, tq, tk, frame)

    def step(masked):
        mask = _mask(q0, k0, tq, tk, frame) if masked else None
        for h in range(heads):
            s = _dot_nt(q_ref[:, h * LANES:(h + 1) * LANES], k_ref[:, h * LANES:(h + 1) * LANES])
            if masked:
                s = jnp.where(mask, s, NEG_INF)
            _online_softmax_step(s, v_ref[:, h * dv:(h + 1) * dv], h, m_sc, l_sc, acc_sc)

    @pl.when(needed & full)
    def _():
        step(False)

    @pl.when(needed & jnp.logical_not(full))
    def _():
        step(True)

    @pl.when(kj == pl.num_programs(2) - 1)
    def _():
        for h in range(heads):
            o_ref[:, h * dv:(h + 1) * dv] = (acc_sc[h] / l_sc[h][:, :dv]).astype(o_ref.dtype)


def _diff_flash_kernel(lam_ref, g_ref, q_ref, k_ref, v_ref, b_ref, o_ref, m_sc, l_sc, acc_sc,
                       *, tq, tk, q_off, lam_init):
    qi, kj = pl.program_id(1), pl.program_id(2)
    q0 = q_off + qi * tq
    k0 = kj * tk

    @pl.when(kj == 0)
    def _():
        m_sc[...] = jnp.full(m_sc.shape, NEG_INF, F32)
        l_sc[...] = jnp.zeros(l_sc.shape, F32)
        acc_sc[...] = jnp.zeros(acc_sc.shape, F32)

    needed, full = _block_visibility(q0, k0, tq, tk, False)

    def step(masked):
        mask = _mask(q0, k0, tq, tk, False) if masked else None
        lane = lax.broadcasted_iota(jnp.int32, (tq, LANES), 1)
        for h in range(DIFF_HEADS):
            q = q_ref[:, h * LANES:(h + 1) * LANES]
            k = k_ref[:, h * LANES:(h + 1) * LANES]
            v = v_ref[:, h * LANES:(h + 1) * LANES]
            bias = b_ref[h]
            zero = jnp.zeros_like(q)
            for c, qc in enumerate((jnp.where(lane < DIFF_QK, q, zero), jnp.where(lane >= DIFF_QK, q, zero))):
                s = _dot_nt(qc, k) + bias
                if masked:
                    s = jnp.where(mask, s, NEG_INF)
                _online_softmax_step(s, v, 2 * h + c, m_sc, l_sc, acc_sc)

    @pl.when(needed & full)
    def _():
        step(False)

    @pl.when(needed & jnp.logical_not(full))
    def _():
        step(True)

    @pl.when(kj == pl.num_programs(2) - 1)
    def _():
        lp = lam_ref[...]
        lam = (jnp.exp(jnp.sum(lp[0:1, :] * lp[1:2, :], axis=1, keepdims=True))
               - jnp.exp(jnp.sum(lp[2:3, :] * lp[3:4, :], axis=1, keepdims=True)) + lam_init)
        g = g_ref[...]
        for h in range(DIFF_HEADS):
            o = acc_sc[2 * h] / l_sc[2 * h] - lam * (acc_sc[2 * h + 1] / l_sc[2 * h + 1])
            o_ref[:, h * LANES:(h + 1) * LANES] = (_rms(o, g) * (1.0 - lam_init)).astype(o_ref.dtype)


def _mix_out_kernel(x_ref, oa_ref, ob_ref, oc_ref, gate_ref, wb_ref, wo_ref, x_out):
    d = x_ref.shape[1]
    mix = None
    for j, o_ref in enumerate((oa_ref, ob_ref, oc_ref)):
        term = gate_ref[:, j * d:(j + 1) * d] * _dot(o_ref[...], wb_ref[j])
        mix = term if mix is None else mix + term
    x_out[...] = x_ref[...] + _dot(mix.astype(BF16), wo_ref[...])


def _ffn_kernel(x_ref, g_ref, win_ref, wout_ref, gfin_ref, x_out, y_out=None):
    x = x_ref[...]
    gu = _dot(_rms(x, g_ref[...]).astype(BF16), win_ref[...])
    dff = wout_ref.shape[0]
    gate, up = gu[:, :dff], gu[:, dff:]
    x2 = x + _dot((gate * jax.nn.sigmoid(gate) * up).astype(BF16), wout_ref[...])
    x_out[...] = x2
    if y_out is not None:
        y_out[...] = _rms(x2, gfin_ref[...])


BF16_SUBLANES = 16
ROW_TILE = 256
MIX_ROW_TILE = 512
PAST_ROW_TILE = 512
PROMPT_ATTN_TILE = 512
SAMPLE_KEY_TILE = 1024


def _row_tile(n, target):
    for t in range(min(n, target), 0, -1):
        if n % t == 0 and (t % BF16_SUBLANES == 0 or t == n):
            return t
    raise ValueError((n, target))


def _full_spec(arr):
    nd = arr.ndim
    return pl.BlockSpec(arr.shape, lambda *_: (0,) * nd)


def _rows3(width, tm):
    return pl.BlockSpec((None, tm, width), lambda b, t: (b, t, 0))


def _mla_in(x, w, tabs, tm):
    B, n, D = x.shape
    hw = MLA_HEADS * LANES
    tab_spec = pl.BlockSpec((tm, LANES), lambda b, t: (t, 0))
    consts = (w["attn_g"], w["mla_in"], w["q_g"], w["uq2"], w["kv_g"], w["ukv2"])
    outs = [(hw, BF16), (hw, BF16), (MLA_HEADS * MLA_V, BF16), (MLA_KV_RANK, F32), (MLA_ROPE, F32)]
    return pl.pallas_call(
        _mla_in_kernel,
        grid=(B, n // tm),
        in_specs=[_rows3(D, tm)] + [_full_spec(a) for a in consts] + [tab_spec] * 4,
        out_specs=[_rows3(wd, tm) for wd, _ in outs],
        out_shape=[jax.ShapeDtypeStruct((B, n, wd), dt) for wd, dt in outs],
        compiler_params=_cparams(("parallel", "parallel")),
        name="mla_in",
    )(x, *consts, *tabs)


def _fox_in(x, w, c0, tm):
    B, n, D = x.shape
    hw = FOX_HEADS * LANES
    wd = FOX_HEADS * FOX_DIM
    consts = (w["attn_g"], w["fox_in"], w["fox_bias"], w["fox_sel"])
    row1 = pl.BlockSpec((None, 1, LANES), lambda b, t: (b, 0, 0))
    outs = [(hw, BF16), (hw, BF16), (wd, BF16), (wd, F32), (wd, F32), (FOX_HEADS, F32)]
    return pl.pallas_call(
        _fox_in_kernel,
        grid=(B, n // tm),
        in_specs=[_rows3(D, tm)] + [_full_spec(a) for a in consts] + [row1],
        out_specs=[_rows3(o, tm) for o, _ in outs] + [row1],
        out_shape=[jax.ShapeDtypeStruct((B, n, o), dt) for o, dt in outs]
        + [jax.ShapeDtypeStruct((B, 1, LANES), F32)],
        scratch_shapes=[pltpu.VMEM((1, LANES), F32)],
        compiler_params=_cparams(("arbitrary", "arbitrary")),
        name="fox_in",
    )(x, *consts, c0)


def _diff_gate_in(x, w, tm):
    B, n, D = x.shape
    wd = DIFF_HEADS * DIFF_V
    consts = (w["attn_g"], w["diff_in"], w["gate_in"])
    outs = [(wd, BF16), (wd, BF16), (wd, BF16), (wd, F32), (wd, F32), (N_BRANCHES * D, F32)]
    return pl.pallas_call(
        _diff_gate_in_kernel,
        grid=(B, n // tm),
        in_specs=[_rows3(D, tm)] + [_full_spec(a) for a in consts],
        out_specs=[_rows3(o, tm) for o, _ in outs],
        out_shape=[jax.ShapeDtypeStruct((B, n, o), dt) for o, dt in outs],
        compiler_params=_cparams(("parallel", "parallel")),
        name="diff_gate_in",
    )(x, *consts)


def _mla_past(lat, kr, w, tm):
    B, n, _ = lat.shape
    hw = MLA_HEADS * LANES
    consts = (w["ukv2"], w["krope_sel"])
    return pl.pallas_call(
        _mla_past_kernel,
        grid=(B, n // tm),
        in_specs=[_rows3(MLA_KV_RANK, tm), _rows3(MLA_ROPE, tm)] + [_full_spec(a) for a in consts],
        out_specs=[_rows3(hw, tm), _rows3(MLA_HEADS * MLA_V, tm)],
        out_shape=[jax.ShapeDtypeStruct((B, n, hw), BF16), jax.ShapeDtypeStruct((B, n, MLA_HEADS * MLA_V), BF16)],
        compiler_params=_cparams(("parallel", "parallel")),
        name="mla_past",
    )(lat, kr, *consts)


def _fox_past(k, lf, w, tm):
    B, n, wd = k.shape
    hw = FOX_HEADS * LANES
    row1 = pl.BlockSpec((None, 1, LANES), lambda b, t: (b, 0, 0))
    return pl.pallas_call(
        _fox_past_kernel,
        grid=(B, n // tm),
        in_specs=[_rows3(wd, tm), _rows3(FOX_HEADS, tm), _full_spec(w["fox_sel"])],
        out_specs=[_rows3(hw, tm), row1],
        out_shape=[jax.ShapeDtypeStruct((B, n, hw), BF16), jax.ShapeDtypeStruct((B, 1, LANES), F32)],
        scratch_shapes=[pltpu.VMEM((1, LANES), F32), pltpu.VMEM((tm, LANES), F32)],
        compiler_params=_cparams(("arbitrary", "arbitrary")),
        name="fox_past",
    )(k, lf, w["fox_sel"])


def _last_needed_block(qi, tq, tk, q_off, nk, frame):
    last_pos = q_off + qi * tq + (tq - 1)
    if not frame:
        last_pos = ((last_pos >> CHUNK_SHIFT) << CHUNK_SHIFT) + ((1 << CHUNK_SHIFT) - 1)
    return jnp.minimum(last_pos // tk, nk - 1)


def _flash(q, k, v, *, heads, dv, tq, tk, q_off, frame, name):
    B, nq_rows, qw = q.shape
    nk_rows = k.shape[1]
    nq, nk = nq_rows // tq, nk_rows // tk

    def kv_map(b, qi, kj):
        return (b, jnp.minimum(kj, _last_needed_block(qi, tq, tk, q_off, nk, frame)), 0)

    kern = functools.partial(_flash_kernel, heads=heads, dv=dv, tq=tq, tk=tk, q_off=q_off, frame=frame)
    return pl.pallas_call(
        kern,
        grid=(B, nq, nk),
        in_specs=[pl.BlockSpec((None, tq, qw), lambda b, qi, kj: (b, qi, 0)),
                  pl.BlockSpec((None, tk, qw), kv_map),
                  pl.BlockSpec((None, tk, heads * dv), kv_map)],
        out_specs=pl.BlockSpec((None, tq, heads * dv), lambda b, qi, kj: (b, qi, 0)),
        out_shape=jax.ShapeDtypeStruct((B, nq_rows, heads * dv), BF16),
        scratch_shapes=[pltpu.VMEM((heads, tq, LANES), F32), pltpu.VMEM((heads, tq, LANES), F32),
                        pltpu.VMEM((heads, tq, dv), F32)],
        compiler_params=_cparams(("parallel", "parallel", "arbitrary")),
        name=name,
    )(q, k, v)


def _diff_flash(q, k, v, bias, lam_p, g_sub, *, tq, tk, q_off, lam_init, n_near):
    B, nq_rows, w = q.shape
    nk_rows = k.shape[1]
    nq, nk = nq_rows // tq, nk_rows // tk
    assert nq == 1 or tq == tk

    def kv_map(b, qi, kj):
        return (b, jnp.minimum(kj, _last_needed_block(qi, tq, tk, q_off, nk, False)), 0)

    def bias_map(b, qi, kj):
        steps = (qi - kj) if nq > 1 else (nk - 1 - kj)
        return (jnp.clip(steps, 0, n_near), 0, 0, 0)

    kern = functools.partial(_diff_flash_kernel, tq=tq, tk=tk, q_off=q_off, lam_init=lam_init)
    return pl.pallas_call(
        kern,
        grid=(B, nq, nk),
        in_specs=[_full_spec(lam_p), _full_spec(g_sub),
                  pl.BlockSpec((None, tq, w), lambda b, qi, kj: (b, qi, 0)),
                  pl.BlockSpec((None, tk, w), kv_map),
                  pl.BlockSpec((None, tk, w), kv_map),
                  pl.BlockSpec((None, DIFF_HEADS, tq, tk), bias_map)],
        out_specs=pl.BlockSpec((None, tq, w), lambda b, qi, kj: (b, qi, 0)),
        out_shape=jax.ShapeDtypeStruct((B, nq_rows, w), BF16),
        scratch_shapes=[pltpu.VMEM((2 * DIFF_HEADS, tq, LANES), F32), pltpu.VMEM((2 * DIFF_HEADS, tq, LANES), F32),
                        pltpu.VMEM((2 * DIFF_HEADS, tq, DIFF_V), F32)],
        compiler_params=_cparams(("parallel", "parallel", "arbitrary")),
        name="diff_flash",
    )(lam_p, g_sub, q, k, v, bias)


def _mix_out(x, oa, ob, oc, gates, w, tm):
    M, D = x.shape
    rows = lambda wd: pl.BlockSpec((tm, wd), lambda t: (t, 0))
    return pl.pallas_call(
        _mix_out_kernel,
        grid=(M // tm,),
        in_specs=[rows(D), rows(MIX_WIDTH), rows(MIX_WIDTH), rows(MIX_WIDTH), rows(N_BRANCHES * D),
                  _full_spec(w["branch"]), _full_spec(w["o"])],
        out_specs=rows(D),
        out_shape=jax.ShapeDtypeStruct((M, D), F32),
        compiler_params=_cparams(("parallel",)),
        name="mix_out",
    )(x, oa, ob, oc, gates, w["branch"], w["o"])


def _ffn(x, w, final_g, tm):
    M, D = x.shape
    rows = pl.BlockSpec((tm, D), lambda t: (t, 0))
    consts = (w["ffn_g"], w["ffn_in"], w["ffn_out"], final_g)
    last = final_g is not None
    if not last:
        consts = consts[:3] + (w["ffn_g"],)
    out_shape = [jax.ShapeDtypeStruct((M, D), F32)] * (2 if last else 1)
    res = pl.pallas_call(
        _ffn_kernel,
        grid=(M // tm,),
        in_specs=[rows] + [_full_spec(a) for a in consts],
        out_specs=[rows] * len(out_shape),
        out_shape=out_shape,
        compiler_params=_cparams(("parallel",)),
        name="ffn_final" if last else "ffn",
    )(x, *consts)
    return res if last else (res[0], None)


def _rot_cols(w):
    half = MLA_ROPE // 2
    return jnp.concatenate([-w[..., half:], w[..., :half]], axis=-1)


def _head_blocks(w, heads, width):
    k = w.shape[0]
    w = w.reshape(k, heads, width)
    return jnp.pad(w, ((0, 0), (0, 0), (0, LANES - width))).reshape(k, heads * LANES)


def _prep_layer_weights(li, p):
    d = p["w_in"].shape[1]
    sizes = (MLA_Q_RANK, MLA_KV_RANK, MLA_ROPE,
             FOX_HEADS * FOX_DIM, FOX_HEADS * FOX_DIM, FOX_HEADS * FOX_DIM, FOX_HEADS,
             DIFF_HEADS * 2 * DIFF_QK, DIFF_HEADS * 2 * DIFF_QK, DIFF_HEADS * DIFF_V, N_BRANCHES * d)
    offs = [0]
    for s in sizes:
        offs.append(offs[-1] + s)
    win = p["w_in"][li]
    col = lambda i: win[:, offs[i]:offs[i + 1]]
    w_cq, w_ckv, w_kr, w_fq, w_fk, w_fv, w_ff, w_dq, w_dk, w_dv, w_gl = (col(i) for i in range(11))

    def rope_block(wr):
        return jnp.pad(wr, ((0, 0), (MLA_NOPE, LANES - MLA_NOPE - MLA_ROPE)))

    mla_in = jnp.concatenate([w_cq, w_ckv, rope_block(w_kr), rope_block(_rot_cols(w_kr))], axis=1)
    uq = p["w_mla_uq"][li].reshape(MLA_Q_RANK, MLA_HEADS, MLA_NOPE + MLA_ROPE)
    uq_nope, uq_rope = uq[..., :MLA_NOPE], uq[..., MLA_NOPE:]
    pad = jnp.zeros((MLA_Q_RANK, MLA_HEADS, LANES - MLA_NOPE - MLA_ROPE), F32)
    uq_cat = jnp.concatenate([uq_nope, uq_rope, pad], axis=-1).reshape(MLA_Q_RANK, MLA_HEADS * LANES)
    uq_rot = jnp.concatenate([jnp.zeros_like(uq_nope), _rot_cols(uq_rope), pad], axis=-1)
    uq2 = jnp.concatenate([uq_cat, uq_rot.reshape(MLA_Q_RANK, MLA_HEADS * LANES)], axis=1)
    ukv = p["w_mla_ukv"][li].reshape(MLA_KV_RANK, MLA_HEADS, MLA_NOPE + MLA_V)
    ukn = _head_blocks(ukv[..., :MLA_NOPE].reshape(MLA_KV_RANK, -1), MLA_HEADS, MLA_NOPE)
    uv = ukv[..., MLA_NOPE:].reshape(MLA_KV_RANK, MLA_HEADS * MLA_V)
    ukv2 = jnp.concatenate([ukn, uv], axis=1)
    fox_in = jnp.concatenate([w_fq, w_fk, w_fv, jnp.pad(w_ff, ((0, 0), (0, LANES - FOX_HEADS)))], axis=1)
    fox_sel = _head_blocks(jnp.eye(FOX_HEADS * FOX_DIM, dtype=F32), FOX_HEADS, FOX_DIM)
    krope_sel = jnp.pad(jnp.eye(MLA_ROPE, dtype=F32), ((0, 0), (MLA_NOPE, LANES - MLA_NOPE - MLA_ROPE)))
    row = lambda a: a.reshape(1, -1)
    return {
        "attn_g": row(p["attn_norm_g"][li]),
        "mla_in": mla_in.astype(BF16),
        "q_g": row(p["mla_q_norm_g"][li]),
        "uq2": uq2.astype(BF16),
        "kv_g": row(p["mla_kv_norm_g"][li]),
        "ukv2": ukv2.astype(BF16),
        "krope_sel": krope_sel.astype(BF16),
        "fox_in": fox_in.astype(BF16),
        "fox_bias": jnp.pad(row(p["fox_f_bias"][li]), ((0, 0), (0, LANES - FOX_HEADS))),
        "fox_sel": fox_sel.astype(BF16),
        "diff_in": jnp.concatenate([w_dq, w_dk, w_dv], axis=1).astype(BF16),
        "gate_in": w_gl.astype(BF16),
        "diff_lambda": p["diff_lambda"][li],
        "diff_g": row(p["diff_subln_g"][li]),
        "branch": p["w_branch"][li].astype(BF16),
        "o": p["w_o"][li].astype(BF16),
        "ffn_g": row(p["ffn_norm_g"][li]),
        "ffn_in": p["w_ffn_in"][li].astype(BF16),
        "ffn_out": p["w_ffn_out"][li].astype(BF16),
    }


def _rope_tables(pos):
    half = MLA_ROPE // 2
    inv = 1.0 / (ROPE_BASE ** (jnp.arange(half, dtype=F32) / half))
    ang = pos.astype(F32)[:, None] * inv[None, :]
    cos, sin = jnp.cos(ang), jnp.sin(ang)
    cos2, sin2 = jnp.concatenate([cos, cos], axis=1), jnp.concatenate([sin, sin], axis=1)
    n = pos.shape[0]
    tail = jnp.zeros((n, LANES - MLA_NOPE - MLA_ROPE), F32)
    scale = (MLA_NOPE + MLA_ROPE) ** -0.5
    cos_q = jnp.concatenate([jnp.ones((n, MLA_NOPE), F32), cos2, tail], axis=1) * scale
    sin_q = jnp.concatenate([jnp.zeros((n, MLA_NOPE), F32), sin2, tail], axis=1) * scale
    cos_k = jnp.concatenate([jnp.zeros((n, MLA_NOPE), F32), cos2, tail], axis=1)
    sin_k = jnp.concatenate([jnp.zeros((n, MLA_NOPE), F32), sin2, tail], axis=1)
    return cos_q, sin_q, cos_k, sin_k


def _rel_bucket(rel):
    half = REL_BUCKETS // 2
    max_exact = half // 2
    ret = jnp.where(rel > 0, half, 0)
    n = jnp.abs(rel)
    large = max_exact + (jnp.log(jnp.maximum(n, 1).astype(F32) / max_exact)
                         / math.log(REL_MAX_DIST / max_exact) * (half - max_exact)).astype(jnp.int32)
    large = jnp.minimum(large, half - 1)
    return ret + jnp.where(n < max_exact, n, large)


def _bias_tiles(rel_table, tq, tk, deltas):
    ii = jnp.arange(tq, dtype=jnp.int32)[:, None]
    jj = jnp.arange(tk, dtype=jnp.int32)[None, :]
    tiles = [jnp.transpose(rel_table[_rel_bucket(jj - ii - d)], (2, 0, 1)) for d in deltas]
    return jnp.stack(tiles).astype(F32)


def _bias_plan(n, n_keys, tq, tk, q_off):
    nq = n // tq
    first = (q_off % tk) if nq == 1 else 0
    step = tk
    far_from = tk + REL_MAX_DIST - 1
    deltas = []
    d = first
    while d < far_from:
        deltas.append(d)
        d += step
    if nq == 1:
        assert q_off + n == n_keys and first == q_off - (n_keys // tk - 1) * tk
    return deltas + [d], len(deltas)


def _run_trunk(x, past, weights, rel_table, final_g):
    B, n, D = x.shape
    depth = len(weights)
    n_past = past["mla_latent"].shape[2] if past else 0
    n_keys = n_past + n
    if past:
        tq, tk = n, _row_tile(n_keys, SAMPLE_KEY_TILE)
    else:
        tq = tk = _row_tile(n, PROMPT_ATTN_TILE)
    tm = _row_tile(n, ROW_TILE)
    pos = n_past + jnp.arange(n, dtype=jnp.int32)
    tabs = _rope_tables(pos)
    deltas, n_near = _bias_plan(n, n_keys, tq, tk, n_past)
    bias = _bias_tiles(rel_table, tq, tk, deltas)
    zero_carry = jnp.zeros((B, 1, LANES), F32)

    new = {k: [] for k in ("mla_latent", "mla_krope", "fox_k", "fox_v", "fox_logf", "diff_k", "diff_v")}
    y = None
    for li in range(depth):
        w = weights[li]
        q_a, k_a, v_a, lat, krope = _mla_in(x, w, tabs, tm)
        carry = zero_carry
        if past:
            tp = _row_tile(n_past, PAST_ROW_TILE)
            k_a_p, v_a_p = _mla_past(past["mla_latent"][li], past["mla_krope"][li], w, tp)
            fk_p, carry = _fox_past(past["fox_k"][li].reshape(B, n_past, -1), past["fox_logf"][li], w, tp)
        q_b, k_b, v_b, fk, fv, logf, _ = _fox_in(x, w, carry, tm)
        q_c, k_c, v_c, dk, dv, gates = _diff_gate_in(x, w, tm)
        if past:
            cat = lambda a, b: jnp.concatenate([a, b], axis=1)
            k_a, v_a = cat(k_a_p, k_a), cat(v_a_p, v_a)
            k_b = cat(fk_p, k_b)
            v_b = cat(past["fox_v"][li].reshape(B, n_past, -1).astype(BF16), v_b)
            k_c = cat(past["diff_k"][li].reshape(B, n_past, -1).astype(BF16), k_c)
            v_c = cat(past["diff_v"][li].reshape(B, n_past, -1).astype(BF16), v_c)
        o_a = _flash(q_a, k_a, v_a, heads=MLA_HEADS, dv=MLA_V, tq=tq, tk=tk, q_off=n_past, frame=False,
                     name="mla_flash")
        o_b = _flash(q_b, k_b, v_b, heads=FOX_HEADS, dv=FOX_DIM, tq=tq, tk=tk, q_off=n_past, frame=True,
                     name="fox_flash")
        lam_init = 0.8 - 0.6 * math.exp(-0.3 * li)
        o_c = _diff_flash(q_c, k_c, v_c, bias, w["diff_lambda"], w["diff_g"], tq=tq, tk=tk, q_off=n_past,
                          lam_init=lam_init, n_near=n_near)
        M = B * n
        flat = lambda a: a.reshape(M, a.shape[-1])
        x1 = _mix_out(flat(x), flat(o_a), flat(o_b), flat(o_c), flat(gates), w, _row_tile(M, MIX_ROW_TILE))
        x2, y = _ffn(x1, w, final_g if li == depth - 1 else None, _row_tile(M, ROW_TILE))
        x = x2.reshape(B, n, D)
        for name, val in (("mla_latent", lat), ("mla_krope", krope), ("fox_k", fk), ("fox_v", fv),
                          ("fox_logf", logf), ("diff_k", dk), ("diff_v", dv)):
            new[name].append(val)
    rows = {
        "mla_latent": jnp.stack(new["mla_latent"]),
        "mla_krope": jnp.stack(new["mla_krope"]),
        "fox_k": jnp.stack(new["fox_k"]).reshape(depth, B, n, FOX_HEADS, FOX_DIM),
        "fox_v": jnp.stack(new["fox_v"]).reshape(depth, B, n, FOX_HEADS, FOX_DIM),
        "fox_logf": jnp.stack(new["fox_logf"]),
        "diff_k": jnp.stack(new["diff_k"]).reshape(depth, B, n, DIFF_HEADS, 2 * DIFF_QK),
        "diff_v": jnp.stack(new["diff_v"]).reshape(depth, B, n, DIFF_HEADS, DIFF_V),
    }
    return y.reshape(B, n, D), rows


def kernel(x_prompt, x_sample, cache_mla_latent, cache_mla_krope, cache_fox_k, cache_fox_v, cache_fox_logf,
           cache_diff_k, cache_diff_v, attn_norm_g, w_in, mla_q_norm_g, mla_kv_norm_g, w_mla_uq, w_mla_ukv,
           fox_f_bias, diff_lambda, diff_subln_g, w_branch, w_o, ffn_norm_g, w_ffn_in, w_ffn_out,
           rel_bias_table, final_norm_g):
    params = {
        "attn_norm_g": attn_norm_g, "w_in": w_in, "mla_q_norm_g": mla_q_norm_g,
        "mla_kv_norm_g": mla_kv_norm_g, "w_mla_uq": w_mla_uq, "w_mla_ukv": w_mla_ukv,
        "fox_f_bias": fox_f_bias, "diff_lambda": diff_lambda, "diff_subln_g": diff_subln_g,
        "w_branch": w_branch, "w_o": w_o, "ffn_norm_g": ffn_norm_g, "w_ffn_in": w_ffn_in,
        "w_ffn_out": w_ffn_out,
    }
    past = {
        "mla_latent": cache_mla_latent, "mla_krope": cache_mla_krope, "fox_k": cache_fox_k,
        "fox_v": cache_fox_v, "fox_logf": cache_fox_logf, "diff_k": cache_diff_k, "diff_v": cache_diff_v,
    }
    depth = w_in.shape[0]
    weights = [_prep_layer_weights(li, params) for li in range(depth)]
    final_g = final_norm_g.reshape(1, -1)
    y_p, sp = _run_trunk(x_prompt, {}, weights, rel_bias_table, final_g)
    y_s, ss = _run_trunk(x_sample, past, weights, rel_bias_table, final_g)
    return (y_p, y_s,
            sp["mla_latent"], ss["mla_latent"], sp["mla_krope"], ss["mla_krope"],
            sp["fox_k"], ss["fox_k"], sp["fox_v"], ss["fox_v"],
            sp["fox_logf"], ss["fox_logf"], sp["diff_k"], ss["diff_k"],
            sp["diff_v"], ss["diff_v"])
```

```python
import functools
import math

import jax
import jax.numpy as jnp
from jax import lax
from jax.experimental import pallas as pl
from jax.experimental.pallas import tpu as pltpu

F32 = jnp.float32
BF16 = jnp.bfloat16

CHUNK_SHIFT = 6
NORM_EPS = 1e-6
LOG2E = math.log2(math.e)
NEG_INF = -1e30
MLA_HEADS, MLA_NOPE, MLA_ROPE, MLA_V = 8, 64, 32, 64
MLA_Q_RANK, MLA_KV_RANK = 384, 256
ROPE_BASE = 10000.0
FOX_HEADS, FOX_DIM = 8, 64
DIFF_HEADS, DIFF_QK, DIFF_V = 4, 64, 128
MIX_WIDTH = 512
N_BRANCHES = 3
REL_BUCKETS, REL_MAX_DIST = 32, 128

LANES = 128
VMEM_LIMIT = 56 * 1024 * 1024


def _cparams(sem):
    return pltpu.CompilerParams(dimension_semantics=sem, vmem_limit_bytes=VMEM_LIMIT)


def _rms(x, g):
    return x * lax.rsqrt(jnp.mean(x * x, axis=-1, keepdims=True) + NORM_EPS) * g


def _dot(a, b):
    return jnp.dot(a, b, preferred_element_type=F32)


def _dot_nt(a, b):
    return lax.dot_general(a, b, (((1,), (1,)), ((), ())), preferred_element_type=F32)


def _bf16_part(x):
    return x.astype(BF16).astype(F32)


def _store_v(v_out, v, transposed):
    v_out[...] = (jnp.transpose(v) if transposed else v).astype(BF16)


def _cumsum_rows(x):
    rows = x.shape[0]
    row = lax.broadcasted_iota(jnp.int32, x.shape, 0)
    shift = 1
    while shift < rows:
        x = x + jnp.where(row >= shift, pltpu.roll(x, shift, 0), 0.0)
        shift *= 2
    return x


def _mla_in_kernel(x_ref, ga_ref, win_ref, gq_ref, wuq_ref, gkv_ref, wukv_ref,
                   cq_ref, sq_ref, ck_ref, sk_ref,
                   q_out, k_out, v_out, lat_out, kr_out, *, v_transposed):
    hb = _rms(x_ref[...], ga_ref[...]).astype(BF16)
    pm = _dot(hb, win_ref[...])
    cqn = _rms(pm[:, :MLA_Q_RANK], gq_ref[...]).astype(BF16)
    q2 = _dot(cqn, wuq_ref[...])
    hw = MLA_HEADS * LANES
    cosq = jnp.concatenate([cq_ref[...]] * MLA_HEADS, axis=1)
    sinq = jnp.concatenate([sq_ref[...]] * MLA_HEADS, axis=1)
    q_out[...] = (q2[:, :hw] * cosq + q2[:, hw:] * sinq).astype(BF16)
    latent = _rms(pm[:, MLA_Q_RANK:MLA_Q_RANK + MLA_KV_RANK], gkv_ref[...])
    lat_out[...] = latent
    kv2 = _dot(latent.astype(BF16), wukv_ref[...])
    o = MLA_Q_RANK + MLA_KV_RANK
    krb = pm[:, o:o + LANES] * ck_ref[...] + pm[:, o + LANES:o + 2 * LANES] * sk_ref[...]
    kr_out[...] = krb[:, MLA_NOPE:MLA_NOPE + MLA_ROPE]
    k_out[...] = (kv2[:, :hw] + jnp.concatenate([krb] * MLA_HEADS, axis=1)).astype(BF16)
    _store_v(v_out, kv2[:, hw:], v_transposed)


def _decay_columns(c, rows):
    hi = _bf16_part(c)
    r1 = c - hi
    mid = _bf16_part(r1)
    lo = _bf16_part(r1 - mid)
    lane = lax.broadcasted_iota(jnp.int32, (rows, LANES), 1)
    d = FOX_DIM
    ones_q = jnp.where((lane >= d + 3) & (lane < d + 6), 1.0, 0.0)
    ones_k = jnp.where((lane >= d) & (lane < d + 3), 1.0, 0.0)
    eq, ek = [], []
    for h in range(FOX_HEADS):
        bh = jnp.broadcast_to(hi[:, h:h + 1], (rows, LANES))
        bm = jnp.broadcast_to(mid[:, h:h + 1], (rows, LANES))
        bl = jnp.broadcast_to(lo[:, h:h + 1], (rows, LANES))
        eq.append(jnp.where(lane == d, bh, jnp.where(lane == d + 1, bm, jnp.where(lane == d + 2, bl, ones_q))))
        ek.append(jnp.where(lane == d + 3, -bh, jnp.where(lane == d + 4, -bm, jnp.where(lane == d + 5, -bl, ones_k))))
    return jnp.concatenate(eq, axis=1), jnp.concatenate(ek, axis=1)


def _fox_in_kernel(x_ref, ga_ref, win_ref, fb_ref, sel_ref, c0_ref,
                   q_out, k_out, vb_out, kf_out, vf_out, lf_out, ct_out, carry_sc, *, v_transposed):
    rows = x_ref.shape[0]
    t = pl.program_id(1)

    @pl.when(t == 0)
    def _():
        carry_sc[...] = c0_ref[...]

    hb = _rms(x_ref[...], ga_ref[...]).astype(BF16)
    pf = _dot(hb, win_ref[...])
    w = FOX_HEADS * FOX_DIM
    fq = pf[:, :w] * (FOX_DIM ** -0.5 * LOG2E)
    fk = pf[:, w:2 * w]
    fv = pf[:, 2 * w:3 * w]
    z = pf[:, 3 * w:3 * w + LANES] + fb_ref[...]
    lane = lax.broadcasted_iota(jnp.int32, (rows, LANES), 1)
    logf = jnp.minimum(z, 0.0) - jnp.log1p(jnp.exp(-jnp.abs(z)))
    logf = jnp.where(lane < FOX_HEADS, logf, 0.0)
    lf_out[...] = logf[:, :FOX_HEADS]
    c = carry_sc[...] + _cumsum_rows(logf)
    carry_sc[...] = c[rows - 1:rows, :]
    ct_out[...] = c[rows - 1:rows, :]
    kf_out[...] = fk
    vf_out[...] = fv
    _store_v(vb_out, fv, v_transposed)
    eq, ek = _decay_columns(c * LOG2E, rows)
    q_out[...] = (_dot(fq.astype(BF16), sel_ref[...]) + eq).astype(BF16)
    k_out[...] = (_dot(fk.astype(BF16), sel_ref[...]) + ek).astype(BF16)


def _diff_gate_in_kernel(x_ref, ga_ref, wd_ref, wg_ref,
                         dq_out, dkb_out, dvb_out, dk_out, dv_out, gate_out, *, v_transposed):
    hb = _rms(x_ref[...], ga_ref[...]).astype(BF16)
    pd = _dot(hb, wd_ref[...])
    w = DIFF_HEADS * 2 * DIFF_QK
    dq_out[...] = (pd[:, :w] * (DIFF_QK ** -0.5 * LOG2E)).astype(BF16)
    dk = pd[:, w:2 * w]
    dv = pd[:, 2 * w:]
    dk_out[...] = dk
    dv_out[...] = dv
    dkb_out[...] = dk.astype(BF16)
    _store_v(dvb_out, dv, v_transposed)
    gate_out[...] = jax.nn.sigmoid(_dot(hb, wg_ref[...]))


def _mla_past_kernel(lat_ref, kr_ref, wukv_ref, selk_ref, k_out, v_out):
    kv2 = _dot(lat_ref[...].astype(BF16), wukv_ref[...])
    krb = _dot(kr_ref[...].astype(BF16), selk_ref[...])
    hw = MLA_HEADS * LANES
    k_out[...] = (kv2[:, :hw] + jnp.concatenate([krb] * MLA_HEADS, axis=1)).astype(BF16)
    v_out[...] = kv2[:, hw:].astype(BF16)


def _fox_past_kernel(k_ref, lf_ref, sel_ref, k_out, ct_out, carry_sc, lf_sc):
    rows = k_ref.shape[0]
    t = pl.program_id(1)

    @pl.when(t == 0)
    def _():
        carry_sc[...] = jnp.zeros(carry_sc.shape, F32)

    lf_sc[...] = jnp.zeros(lf_sc.shape, F32)
    lf_sc[:, :FOX_HEADS] = lf_ref[...]
    c = carry_sc[...] + _cumsum_rows(lf_sc[...])
    carry_sc[...] = c[rows - 1:rows, :]
    ct_out[...] = c[rows - 1:rows, :]
    _, ek = _decay_columns(c * LOG2E, rows)
    k_out[...] = (_dot(k_ref[...].astype(BF16), sel_ref[...]) + ek).astype(BF16)


def _block_visibility(q0, k0, tq, tk, frame):
    if frame:
        return k0 <= q0 + (tq - 1), k0 + (tk - 1) <= q0
    return ((k0 >> CHUNK_SHIFT) <= ((q0 + (tq - 1)) >> CHUNK_SHIFT),
            ((k0 + (tk - 1)) >> CHUNK_SHIFT) <= (q0 >> CHUNK_SHIFT))


def _mask(q0, k0, tq, tk, frame):
    row = q0 + lax.broadcasted_iota(jnp.int32, (tq, tk), 0)
    col = k0 + lax.broadcasted_iota(jnp.int32, (tq, tk), 1)
    if frame:
        return col <= row
    return (col >> CHUNK_SHIFT) <= (row >> CHUNK_SHIFT)


def _online_softmax_step(s, v, idx, m_sc, l_sc, acc_sc):
    dv = v.shape[1]
    m_prev = m_sc[idx]
    m_new = jnp.maximum(m_prev, jnp.max(s, axis=1, keepdims=True))
    alpha = jnp.exp2(m_prev - m_new)
    p = jnp.exp2(s - m_new[:, :1])
    l_sc[idx] = alpha * l_sc[idx] + jnp.sum(p, axis=1, keepdims=True)
    m_sc[idx] = m_new
    acc_sc[idx] = acc_sc[idx] * alpha[:, :dv] + _dot(p.astype(BF16), v)


def _init_softmax_state(m_sc, l_sc, acc_sc):
    m_sc[...] = jnp.full(m_sc.shape, NEG_INF, F32)
    l_sc[...] = jnp.zeros(l_sc.shape, F32)
    acc_sc[...] = jnp.zeros(acc_sc.shape, F32)


def _flash_kernel(q_ref, kp_ref, vp_ref, kn_ref, vn_ref, o_ref, m_sc, l_sc, acc_sc, *, heads, dv, n_past, frame):
    kj = pl.program_id(1)
    n_hist = pl.num_programs(1) - 1
    tq = q_ref.shape[0]

    @pl.when(kj == 0)
    def _():
        _init_softmax_state(m_sc, l_sc, acc_sc)

    def step(k_ref, v_ref, mask):
        for h in range(heads):
            s = _dot_nt(q_ref[:, h * LANES:(h + 1) * LANES], k_ref[:, h * LANES:(h + 1) * LANES].astype(BF16))
            if mask is not None:
                s = jnp.where(mask, s, NEG_INF)
            _online_softmax_step(s, v_ref[:, h * dv:(h + 1) * dv].astype(BF16), h, m_sc, l_sc, acc_sc)

    @pl.when(kj < n_hist)
    def _():
        step(kp_ref, vp_ref, None)

    @pl.when(kj == n_hist)
    def _():
        step(kn_ref, vn_ref, _mask(n_past, n_past, tq, kn_ref.shape[0], frame))
        for h in range(heads):
            o_ref[:, h * dv:(h + 1) * dv] = (acc_sc[h] / l_sc[h][:, :dv]).astype(o_ref.dtype)


def _diff_flash_kernel(lam_ref, g_ref, q_ref, kp_ref, vp_ref, kn_ref, vn_ref, bp_ref, bn_ref, o_ref,
                       m_sc, l_sc, acc_sc, *, n_past, lam_init):
    kj = pl.program_id(1)
    n_hist = pl.num_programs(1) - 1
    tq = q_ref.shape[0]

    @pl.when(kj == 0)
    def _():
        _init_softmax_state(m_sc, l_sc, acc_sc)

    def step(k_ref, v_ref, b_ref, mask):
        lane = lax.broadcasted_iota(jnp.int32, (tq, LANES), 1)
        for h in range(DIFF_HEADS):
            q = q_ref[:, h * LANES:(h + 1) * LANES]
            k = k_ref[:, h * LANES:(h + 1) * LANES].astype(BF16)
            v = v_ref[:, h * LANES:(h + 1) * LANES].astype(BF16)
            bias = b_ref[h]
            zero = jnp.zeros_like(q)
            for c, qc in enumerate((jnp.where(lane < DIFF_QK, q, zero), jnp.where(lane >= DIFF_QK, q, zero))):
                s = _dot_nt(qc, k) + bias
                if mask is not None:
                    s = jnp.where(mask, s, NEG_INF)
                _online_softmax_step(s, v, 2 * h + c, m_sc, l_sc, acc_sc)

    @pl.when(kj < n_hist)
    def _():
        step(kp_ref, vp_ref, bp_ref, None)

    @pl.when(kj == n_hist)
    def _():
        step(kn_ref, vn_ref, bn_ref, _mask(n_past, n_past, tq, kn_ref.shape[0], False))
        lp = lam_ref[...]
        lam = (jnp.exp(jnp.sum(lp[0:1, :] * lp[1:2, :], axis=1, keepdims=True))
               - jnp.exp(jnp.sum(lp[2:3, :] * lp[3:4, :], axis=1, keepdims=True)) + lam_init)
        g = g_ref[...]
        for h in range(DIFF_HEADS):
            o = acc_sc[2 * h] / l_sc[2 * h] - lam * (acc_sc[2 * h + 1] / l_sc[2 * h + 1])
            o_ref[:, h * LANES:(h + 1) * LANES] = (_rms(o, g) * (1.0 - lam_init)).astype(o_ref.dtype)


def _mask_t(q0, k0, tq, tk, frame):
    key = k0 + lax.broadcasted_iota(jnp.int32, (tk, tq), 0)
    qry = q0 + lax.broadcasted_iota(jnp.int32, (tk, tq), 1)
    if frame:
        return key <= qry
    return (key >> CHUNK_SHIFT) <= (qry >> CHUNK_SHIFT)


def _online_softmax_step_t(st, vt, idx, m_sc, acc_sc):
    m_prev = m_sc[idx]
    m_new = jnp.maximum(m_prev, jnp.max(st, axis=0, keepdims=True))
    alpha = jnp.exp2(m_prev - m_new)
    p = jnp.exp2(st - m_new).astype(BF16)
    m_sc[idx] = m_new
    acc_sc[idx] = acc_sc[idx] * alpha + _dot(vt, p)


def _values_with_ones(vt_ref, h, dv, tk):
    return jnp.concatenate([vt_ref[h * dv:(h + 1) * dv, :], jnp.ones((BF16_SUBLANES, tk), BF16)], axis=0)


def _normalised(acc, dv):
    return acc[:dv, :] / acc[dv:dv + 1, :]


def _flash_t_kernel(q_ref, k_ref, vt_ref, o_ref, m_sc, acc_sc, *, heads, dv, tq, tk, frame):
    qi, kj = pl.program_id(1), pl.program_id(2)
    q0 = qi * tq
    k0 = kj * tk

    @pl.when(kj == 0)
    def _():
        m_sc[...] = jnp.full(m_sc.shape, NEG_INF, F32)
        acc_sc[...] = jnp.zeros(acc_sc.shape, F32)

    needed, full = _block_visibility(q0, k0, tq, tk, frame)

    def step(masked):
        mask = _mask_t(q0, k0, tq, tk, frame) if masked else None
        scores = lambda h: _dot_nt(k_ref[:, h * LANES:(h + 1) * LANES], q_ref[:, h * LANES:(h + 1) * LANES])
        st_next = scores(0)
        for h in range(heads):
            st = st_next
            if h + 1 < heads:
                st_next = scores(h + 1)
            if masked:
                st = jnp.where(mask, st, NEG_INF)
            _online_softmax_step_t(st, _values_with_ones(vt_ref, h, dv, tk), h, m_sc, acc_sc)

    @pl.when(needed & full)
    def _():
        step(False)

    @pl.when(needed & jnp.logical_not(full))
    def _():
        step(True)

    @pl.when(kj == pl.num_programs(2) - 1)
    def _():
        per_blk = LANES // dv
        for g in range(heads // per_blk):
            rows = [_normalised(acc_sc[g * per_blk + i], dv) for i in range(per_blk)]
            blk = rows[0] if per_blk == 1 else jnp.concatenate(rows, axis=0)
            o_ref[:, g * LANES:(g + 1) * LANES] = jnp.transpose(blk).astype(o_ref.dtype)


def _diff_flash_t_kernel(lam_ref, g_ref, q_ref, k_ref, vt_ref, b_ref, o_ref, m_sc, acc_sc,
                         *, tq, tk, lam_init):
    qi, kj = pl.program_id(1), pl.program_id(2)
    q0 = qi * tq
    k0 = kj * tk

    @pl.when(kj == 0)
    def _():
        m_sc[...] = jnp.full(m_sc.shape, NEG_INF, F32)
        acc_sc[...] = jnp.zeros(acc_sc.shape, F32)

    needed, full = _block_visibility(q0, k0, tq, tk, False)

    def step(masked):
        mask = _mask_t(q0, k0, tq, tk, False) if masked else None
        lane = lax.broadcasted_iota(jnp.int32, (tq, LANES), 1)

        def scores(i):
            h, c = divmod(i, 2)
            q = q_ref[:, h * LANES:(h + 1) * LANES]
            qc = jnp.where((lane < DIFF_QK) if c == 0 else (lane >= DIFF_QK), q, jnp.zeros_like(q))
            return _dot_nt(k_ref[:, h * LANES:(h + 1) * LANES], qc)

        n_maps = 2 * DIFF_HEADS
        st_next = scores(0)
        for i in range(n_maps):
            st = st_next + b_ref[i // 2]
            if i + 1 < n_maps:
                st_next = scores(i + 1)
            if masked:
                st = jnp.where(mask, st, NEG_INF)
            _online_softmax_step_t(st, _values_with_ones(vt_ref, i // 2, DIFF_V, tk), i, m_sc, acc_sc)

    @pl.when(needed & full)
    def _():
        step(False)

    @pl.when(needed & jnp.logical_not(full))
    def _():
        step(True)

    @pl.when(kj == pl.num_programs(2) - 1)
    def _():
        lp = lam_ref[...]
        lam = (jnp.exp(jnp.sum(lp[0:1, :] * lp[1:2, :], axis=1, keepdims=True))
               - jnp.exp(jnp.sum(lp[2:3, :] * lp[3:4, :], axis=1, keepdims=True)) + lam_init)
        g = g_ref[...]
        for h in range(DIFF_HEADS):
            o_t = _normalised(acc_sc[2 * h], DIFF_V) - lam * _normalised(acc_sc[2 * h + 1], DIFF_V)
            o = jnp.transpose(o_t)
            o_ref[:, h * LANES:(h + 1) * LANES] = (_rms(o, g) * (1.0 - lam_init)).astype(o_ref.dtype)


def _mix_out_kernel(x_ref, oa_ref, ob_ref, oc_ref, gate_ref, wb_ref, wo_ref, x_out):
    d = x_ref.shape[1]
    mix = None
    for j, o_ref in enumerate((oa_ref, ob_ref, oc_ref)):
        term = gate_ref[:, j * d:(j + 1) * d] * _dot(o_ref[...], wb_ref[j])
        mix = term if mix is None else mix + term
    x_out[...] = x_ref[...] + _dot(mix.astype(BF16), wo_ref[...])


def _ffn_kernel(x_ref, g_ref, win_ref, wout_ref, gfin_ref, x_out, y_out=None):
    x = x_ref[...]
    gu = _dot(_rms(x, g_ref[...]).astype(BF16), win_ref[...])
    dff = wout_ref.shape[0]
    gate, up = gu[:, :dff], gu[:, dff:]
    x2 = x + _dot((gate * jax.nn.sigmoid(gate) * up).astype(BF16), wout_ref[...])
    x_out[...] = x2
    if y_out is not None:
        y_out[...] = _rms(x2, gfin_ref[...])


BF16_SUBLANES = 16
ROW_TILE = 256
MIX_ROW_TILE = 512
PAST_ROW_TILE = 512
PROMPT_Q_TILE = 512
PROMPT_K_TILE = 512
SAMPLE_KEY_TILE = 512


def _row_tile(n, target):
    for t in range(min(n, target), 0, -1):
        if n % t == 0 and (t % BF16_SUBLANES == 0 or t == n):
            return t
    raise ValueError((n, target))


def _full_spec(arr):
    nd = arr.ndim
    return pl.BlockSpec(arr.shape, lambda *_: (0,) * nd)


def _rows3(width, tm):
    return pl.BlockSpec((None, tm, width), lambda b, t: (b, t, 0))


V_OUT = 2


def _in_out_specs(outs, B, n, tm, v_t):
    specs = [_rows3(wd, tm) for wd, _ in outs]
    shapes = [jax.ShapeDtypeStruct((B, n, wd), dt) for wd, dt in outs]
    if v_t:
        wd, dt = outs[V_OUT]
        specs[V_OUT] = pl.BlockSpec((None, wd, tm), lambda b, t: (b, 0, t))
        shapes[V_OUT] = jax.ShapeDtypeStruct((B, wd, n), dt)
    return specs, shapes


def _mla_in(x, w, tabs, tm, v_t):
    B, n, D = x.shape
    hw = MLA_HEADS * LANES
    tab_spec = pl.BlockSpec((tm, LANES), lambda b, t: (t, 0))
    consts = (w["attn_g"], w["mla_in"], w["q_g"], w["uq2"], w["kv_g"], w["ukv2"])
    outs = [(hw, BF16), (hw, BF16), (MLA_HEADS * MLA_V, BF16), (MLA_KV_RANK, F32), (MLA_ROPE, F32)]
    specs, shapes = _in_out_specs(outs, B, n, tm, v_t)
    return pl.pallas_call(
        functools.partial(_mla_in_kernel, v_transposed=v_t),
        grid=(B, n // tm),
        in_specs=[_rows3(D, tm)] + [_full_spec(a) for a in consts] + [tab_spec] * 4,
        out_specs=specs,
        out_shape=shapes,
        compiler_params=_cparams(("parallel", "parallel")),
        name="mla_in",
    )(x, *consts, *tabs)


def _fox_in(x, w, c0, tm, v_t):
    B, n, D = x.shape
    hw = FOX_HEADS * LANES
    wd = FOX_HEADS * FOX_DIM
    consts = (w["attn_g"], w["fox_in"], w["fox_bias"], w["fox_sel"])
    row1 = pl.BlockSpec((None, 1, LANES), lambda b, t: (b, 0, 0))
    outs = [(hw, BF16), (hw, BF16), (wd, BF16), (wd, F32), (wd, F32), (FOX_HEADS, F32)]
    specs, shapes = _in_out_specs(outs, B, n, tm, v_t)
    return pl.pallas_call(
        functools.partial(_fox_in_kernel, v_transposed=v_t),
        grid=(B, n // tm),
        in_specs=[_rows3(D, tm)] + [_full_spec(a) for a in consts] + [row1],
        out_specs=specs + [row1],
        out_shape=shapes + [jax.ShapeDtypeStruct((B, 1, LANES), F32)],
        scratch_shapes=[pltpu.VMEM((1, LANES), F32)],
        compiler_params=_cparams(("arbitrary", "arbitrary")),
        name="fox_in",
    )(x, *consts, c0)


def _diff_gate_in(x, w, tm, v_t):
    B, n, D = x.shape
    wd = DIFF_HEADS * DIFF_V
    consts = (w["attn_g"], w["diff_in"], w["gate_in"])
    outs = [(wd, BF16), (wd, BF16), (wd, BF16), (wd, F32), (wd, F32), (N_BRANCHES * D, F32)]
    specs, shapes = _in_out_specs(outs, B, n, tm, v_t)
    return pl.pallas_call(
        functools.partial(_diff_gate_in_kernel, v_transposed=v_t),
        grid=(B, n // tm),
        in_specs=[_rows3(D, tm)] + [_full_spec(a) for a in consts],
        out_specs=specs,
        out_shape=shapes,
        compiler_params=_cparams(("parallel", "parallel")),
        name="diff_gate_in",
    )(x, *consts)


def _mla_past(lat, kr, w, tm):
    B, n, _ = lat.shape
    hw = MLA_HEADS * LANES
    consts = (w["ukv2"], w["krope_sel"])
    return pl.pallas_call(
        _mla_past_kernel,
        grid=(B, n // tm),
        in_specs=[_rows3(MLA_KV_RANK, tm), _rows3(MLA_ROPE, tm)] + [_full_spec(a) for a in consts],
        out_specs=[_rows3(hw, tm), _rows3(MLA_HEADS * MLA_V, tm)],
        out_shape=[jax.ShapeDtypeStruct((B, n, hw), BF16), jax.ShapeDtypeStruct((B, n, MLA_HEADS * MLA_V), BF16)],
        compiler_params=_cparams(("parallel", "parallel")),
        name="mla_past",
    )(lat, kr, *consts)


def _fox_past(k, lf, w, tm):
    B, n, wd = k.shape
    hw = FOX_HEADS * LANES
    row1 = pl.BlockSpec((None, 1, LANES), lambda b, t: (b, 0, 0))
    return pl.pallas_call(
        _fox_past_kernel,
        grid=(B, n // tm),
        in_specs=[_rows3(wd, tm), _rows3(FOX_HEADS, tm), _full_spec(w["fox_sel"])],
        out_specs=[_rows3(hw, tm), row1],
        out_shape=[jax.ShapeDtypeStruct((B, n, hw), BF16), jax.ShapeDtypeStruct((B, 1, LANES), F32)],
        scratch_shapes=[pltpu.VMEM((1, LANES), F32), pltpu.VMEM((tm, LANES), F32)],
        compiler_params=_cparams(("arbitrary", "arbitrary")),
        name="fox_past",
    )(k, lf, w["fox_sel"])


def _last_needed_block(qi, tq, tk, q_off, nk, frame):
    last_pos = q_off + qi * tq + (tq - 1)
    if not frame:
        last_pos = ((last_pos >> CHUNK_SHIFT) << CHUNK_SHIFT) + ((1 << CHUNK_SHIFT) - 1)
    return jnp.minimum(last_pos // tk, nk - 1)


def _hist_spec(arr, tk, n_hist):
    return pl.BlockSpec((None, tk, arr.shape[2]), lambda b, kj: (b, jnp.minimum(kj, n_hist - 1), 0))


def _per_batch_spec(arr):
    return pl.BlockSpec((None,) + arr.shape[1:], lambda b, kj: (b, 0, 0))


def _flash(q, k_past, v_past, k_new, v_new, *, heads, dv, tk, frame, name):
    B, n, _ = q.shape
    n_past = k_past.shape[1]
    n_hist = n_past // tk
    kern = functools.partial(_flash_kernel, heads=heads, dv=dv, n_past=n_past, frame=frame)
    return pl.pallas_call(
        kern,
        grid=(B, n_hist + 1),
        in_specs=[_per_batch_spec(q), _hist_spec(k_past, tk, n_hist), _hist_spec(v_past, tk, n_hist),
                  _per_batch_spec(k_new), _per_batch_spec(v_new)],
        out_specs=pl.BlockSpec((None, n, heads * dv), lambda b, kj: (b, 0, 0)),
        out_shape=jax.ShapeDtypeStruct((B, n, heads * dv), BF16),
        scratch_shapes=[pltpu.VMEM((heads, n, LANES), F32), pltpu.VMEM((heads, n, LANES), F32),
                        pltpu.VMEM((heads, n, dv), F32)],
        compiler_params=_cparams(("parallel", "arbitrary")),
        name=name,
    )(q, k_past, v_past, k_new, v_new)


def _diff_flash(q, k_past, v_past, k_new, v_new, bias_past, bias_new, lam_p, g_sub, *, tk, lam_init, bias_index):
    B, n, w = q.shape
    n_past = k_past.shape[1]
    n_hist = n_past // tk
    kern = functools.partial(_diff_flash_kernel, n_past=n_past, lam_init=lam_init)
    return pl.pallas_call(
        kern,
        grid=(B, n_hist + 1),
        in_specs=[_full_spec(lam_p), _full_spec(g_sub), _per_batch_spec(q),
                  _hist_spec(k_past, tk, n_hist), _hist_spec(v_past, tk, n_hist),
                  _per_batch_spec(k_new), _per_batch_spec(v_new),
                  pl.BlockSpec((None, DIFF_HEADS, n, tk), lambda b, kj: (bias_index(0, kj), 0, 0, 0)),
                  pl.BlockSpec((None, DIFF_HEADS, n, n), lambda b, kj: (0, 0, 0, 0))],
        out_specs=pl.BlockSpec((None, n, w), lambda b, kj: (b, 0, 0)),
        out_shape=jax.ShapeDtypeStruct((B, n, w), BF16),
        scratch_shapes=[pltpu.VMEM((2 * DIFF_HEADS, n, LANES), F32), pltpu.VMEM((2 * DIFF_HEADS, n, LANES), F32),
                        pltpu.VMEM((2 * DIFF_HEADS, n, DIFF_V), F32)],
        compiler_params=_cparams(("parallel", "arbitrary")),
        name="diff_flash",
    )(lam_p, g_sub, q, k_past, v_past, k_new, v_new, bias_past, bias_new)


def _flash_t(q, k, vt, *, heads, dv, tq, tk, frame, name):
    B, n, qw = q.shape
    nq, nk = n // tq, n // tk

    def last(qi):
        return _last_needed_block(qi, tq, tk, 0, nk, frame)

    kern = functools.partial(_flash_t_kernel, heads=heads, dv=dv, tq=tq, tk=tk, frame=frame)
    return pl.pallas_call(
        kern,
        grid=(B, nq, nk),
        in_specs=[pl.BlockSpec((None, tq, qw), lambda b, qi, kj: (b, qi, 0)),
                  pl.BlockSpec((None, tk, qw), lambda b, qi, kj: (b, jnp.minimum(kj, last(qi)), 0)),
                  pl.BlockSpec((None, heads * dv, tk), lambda b, qi, kj: (b, 0, jnp.minimum(kj, last(qi))))],
        out_specs=pl.BlockSpec((None, tq, heads * dv), lambda b, qi, kj: (b, qi, 0)),
        out_shape=jax.ShapeDtypeStruct((B, n, heads * dv), BF16),
        scratch_shapes=[pltpu.VMEM((heads, 1, tq), F32),
                        pltpu.VMEM((heads, dv + BF16_SUBLANES, tq), F32)],
        compiler_params=_cparams(("parallel", "parallel", "arbitrary")),
        name=name,
    )(q, k, vt)


def _diff_flash_t(q, k, vt, bias, lam_p, g_sub, *, tq, tk, lam_init, bias_index):
    B, n, w = q.shape
    nq, nk = n // tq, n // tk

    def last(qi):
        return _last_needed_block(qi, tq, tk, 0, nk, False)

    kern = functools.partial(_diff_flash_t_kernel, tq=tq, tk=tk, lam_init=lam_init)
    return pl.pallas_call(
        kern,
        grid=(B, nq, nk),
        in_specs=[_full_spec(lam_p), _full_spec(g_sub),
                  pl.BlockSpec((None, tq, w), lambda b, qi, kj: (b, qi, 0)),
                  pl.BlockSpec((None, tk, w), lambda b, qi, kj: (b, jnp.minimum(kj, last(qi)), 0)),
                  pl.BlockSpec((None, w, tk), lambda b, qi, kj: (b, 0, jnp.minimum(kj, last(qi)))),
                  pl.BlockSpec((None, DIFF_HEADS, tk, tq), lambda b, qi, kj: (bias_index(qi, kj), 0, 0, 0))],
        out_specs=pl.BlockSpec((None, tq, w), lambda b, qi, kj: (b, qi, 0)),
        out_shape=jax.ShapeDtypeStruct((B, n, w), BF16),
        scratch_shapes=[pltpu.VMEM((2 * DIFF_HEADS, 1, tq), F32),
                        pltpu.VMEM((2 * DIFF_HEADS, DIFF_V + BF16_SUBLANES, tq), F32)],
        compiler_params=_cparams(("parallel", "parallel", "arbitrary")),
        name="diff_flash_t",
    )(lam_p, g_sub, q, k, vt, bias)


def _mix_out(x, oa, ob, oc, gates, w, tm):
    M, D = x.shape
    rows = lambda wd: pl.BlockSpec((tm, wd), lambda t: (t, 0))
    return pl.pallas_call(
        _mix_out_kernel,
        grid=(M // tm,),
        in_specs=[rows(D), rows(MIX_WIDTH), rows(MIX_WIDTH), rows(MIX_WIDTH), rows(N_BRANCHES * D),
                  _full_spec(w["branch"]), _full_spec(w["o"])],
        out_specs=rows(D),
        out_shape=jax.ShapeDtypeStruct((M, D), F32),
        compiler_params=_cparams(("parallel",)),
        name="mix_out",
    )(x, oa, ob, oc, gates, w["branch"], w["o"])


def _ffn(x, w, final_g, tm):
    M, D = x.shape
    rows = pl.BlockSpec((tm, D), lambda t: (t, 0))
    consts = (w["ffn_g"], w["ffn_in"], w["ffn_out"], final_g)
    last = final_g is not None
    if not last:
        consts = consts[:3] + (w["ffn_g"],)
    out_shape = [jax.ShapeDtypeStruct((M, D), F32)] * (2 if last else 1)
    res = pl.pallas_call(
        _ffn_kernel,
        grid=(M // tm,),
        in_specs=[rows] + [_full_spec(a) for a in consts],
        out_specs=[rows] * len(out_shape),
        out_shape=out_shape,
        compiler_params=_cparams(("parallel",)),
        name="ffn_final" if last else "ffn",
    )(x, *consts)
    return res if last else (res[0], None)


def _rot_cols(w):
    half = MLA_ROPE // 2
    return jnp.concatenate([-w[..., half:], w[..., :half]], axis=-1)


def _head_blocks(w, heads, width):
    k = w.shape[0]
    w = w.reshape(k, heads, width)
    return jnp.pad(w, ((0, 0), (0, 0), (0, LANES - width))).reshape(k, heads * LANES)


def _prep_layer_weights(li, p):
    d = p["w_in"].shape[1]
    sizes = (MLA_Q_RANK, MLA_KV_RANK, MLA_ROPE,
             FOX_HEADS * FOX_DIM, FOX_HEADS * FOX_DIM, FOX_HEADS * FOX_DIM, FOX_HEADS,
             DIFF_HEADS * 2 * DIFF_QK, DIFF_HEADS * 2 * DIFF_QK, DIFF_HEADS * DIFF_V, N_BRANCHES * d)
    offs = [0]
    for s in sizes:
        offs.append(offs[-1] + s)
    win = p["w_in"][li]
    col = lambda i: win[:, offs[i]:offs[i + 1]]
    w_cq, w_ckv, w_kr, w_fq, w_fk, w_fv, w_ff, w_dq, w_dk, w_dv, w_gl = (col(i) for i in range(11))

    def rope_block(wr):
        return jnp.pad(wr, ((0, 0), (MLA_NOPE, LANES - MLA_NOPE - MLA_ROPE)))

    mla_in = jnp.concatenate([w_cq, w_ckv, rope_block(w_kr), rope_block(_rot_cols(w_kr))], axis=1)
    uq = p["w_mla_uq"][li].reshape(MLA_Q_RANK, MLA_HEADS, MLA_NOPE + MLA_ROPE)
    uq_nope, uq_rope = uq[..., :MLA_NOPE], uq[..., MLA_NOPE:]
    pad = jnp.zeros((MLA_Q_RANK, MLA_HEADS, LANES - MLA_NOPE - MLA_ROPE), F32)
    uq_cat = jnp.concatenate([uq_nope, uq_rope, pad], axis=-1).reshape(MLA_Q_RANK, MLA_HEADS * LANES)
    uq_rot = jnp.concatenate([jnp.zeros_like(uq_nope), _rot_cols(uq_rope), pad], axis=-1)
    uq2 = jnp.concatenate([uq_cat, uq_rot.reshape(MLA_Q_RANK, MLA_HEADS * LANES)], axis=1)
    ukv = p["w_mla_ukv"][li].reshape(MLA_KV_RANK, MLA_HEADS, MLA_NOPE + MLA_V)
    ukn = _head_blocks(ukv[..., :MLA_NOPE].reshape(MLA_KV_RANK, -1), MLA_HEADS, MLA_NOPE)
    uv = ukv[..., MLA_NOPE:].reshape(MLA_KV_RANK, MLA_HEADS * MLA_V)
    ukv2 = jnp.concatenate([ukn, uv], axis=1)
    fox_in = jnp.concatenate([w_fq, w_fk, w_fv, jnp.pad(w_ff, ((0, 0), (0, LANES - FOX_HEADS)))], axis=1)
    fox_sel = _head_blocks(jnp.eye(FOX_HEADS * FOX_DIM, dtype=F32), FOX_HEADS, FOX_DIM)
    krope_sel = jnp.pad(jnp.eye(MLA_ROPE, dtype=F32), ((0, 0), (MLA_NOPE, LANES - MLA_NOPE - MLA_ROPE)))
    row = lambda a: a.reshape(1, -1)
    return {
        "attn_g": row(p["attn_norm_g"][li]),
        "mla_in": mla_in.astype(BF16),
        "q_g": row(p["mla_q_norm_g"][li]),
        "uq2": uq2.astype(BF16),
        "kv_g": row(p["mla_kv_norm_g"][li]),
        "ukv2": ukv2.astype(BF16),
        "krope_sel": krope_sel.astype(BF16),
        "fox_in": fox_in.astype(BF16),
        "fox_bias": jnp.pad(row(p["fox_f_bias"][li]), ((0, 0), (0, LANES - FOX_HEADS))),
        "fox_sel": fox_sel.astype(BF16),
        "diff_in": jnp.concatenate([w_dq, w_dk, w_dv], axis=1).astype(BF16),
        "gate_in": w_gl.astype(BF16),
        "diff_lambda": p["diff_lambda"][li],
        "diff_g": row(p["diff_subln_g"][li]),
        "branch": p["w_branch"][li].astype(BF16),
        "o": p["w_o"][li].astype(BF16),
        "ffn_g": row(p["ffn_norm_g"][li]),
        "ffn_in": p["w_ffn_in"][li].astype(BF16),
        "ffn_out": p["w_ffn_out"][li].astype(BF16),
    }


def _rope_tables(pos):
    half = MLA_ROPE // 2
    inv = 1.0 / (ROPE_BASE ** (jnp.arange(half, dtype=F32) / half))
    ang = pos.astype(F32)[:, None] * inv[None, :]
    cos, sin = jnp.cos(ang), jnp.sin(ang)
    cos2, sin2 = jnp.concatenate([cos, cos], axis=1), jnp.concatenate([sin, sin], axis=1)
    n = pos.shape[0]
    tail = jnp.zeros((n, LANES - MLA_NOPE - MLA_ROPE), F32)
    scale = (MLA_NOPE + MLA_ROPE) ** -0.5 * LOG2E
    cos_q = jnp.concatenate([jnp.ones((n, MLA_NOPE), F32), cos2, tail], axis=1) * scale
    sin_q = jnp.concatenate([jnp.zeros((n, MLA_NOPE), F32), sin2, tail], axis=1) * scale
    cos_k = jnp.concatenate([jnp.zeros((n, MLA_NOPE), F32), cos2, tail], axis=1)
    sin_k = jnp.concatenate([jnp.zeros((n, MLA_NOPE), F32), sin2, tail], axis=1)
    return cos_q, sin_q, cos_k, sin_k


def _rel_bucket(rel):
    half = REL_BUCKETS // 2
    max_exact = half // 2
    ret = jnp.where(rel > 0, half, 0)
    n = jnp.abs(rel)
    large = max_exact + (jnp.log(jnp.maximum(n, 1).astype(F32) / max_exact)
                         / math.log(REL_MAX_DIST / max_exact) * (half - max_exact)).astype(jnp.int32)
    large = jnp.minimum(large, half - 1)
    return ret + jnp.where(n < max_exact, n, large)


def _bias_tiles(rel_table, tq, tk, deltas, key_major):
    ii = jnp.arange(tq, dtype=jnp.int32)
    jj = jnp.arange(tk, dtype=jnp.int32)
    rel0 = (jj[:, None] - ii[None, :]) if key_major else (jj[None, :] - ii[:, None])
    table = rel_table.astype(F32) * LOG2E
    tiles = []
    for d in deltas:
        bucket = _rel_bucket(rel0 - d)[None]
        tile = jnp.zeros((DIFF_HEADS,) + rel0.shape, F32)
        for b in range(REL_BUCKETS):
            tile = jnp.where(bucket == b, table[b][:, None, None], tile)
        tiles.append(tile)
    return jnp.stack(tiles)


def _bias_plan(n, n_keys, tq, tk, q_off):
    nq, nk = n // tq, n_keys // tk
    step = math.gcd(tq, tk) if nq > 1 else tk
    needed = [q_off + qi * tq - kj * tk for qi in range(nq) for kj in range(nk)
              if ((kj * tk) >> CHUNK_SHIFT) <= ((q_off + qi * tq + tq - 1) >> CHUNK_SHIFT)]
    d_min = min(needed)
    far_from = tk + REL_MAX_DIST - 1
    deltas = list(range(d_min, max(far_from, d_min + 1), step))
    n_near = len(deltas)
    deltas.append(deltas[-1] + step)
    base = (q_off - d_min) // step

    def index(qi, kj):
        return jnp.clip(base + qi * (tq // step) - kj * (tk // step), 0, n_near)

    return deltas, index


def _run_trunk(x, past, weights, rel_table, final_g):
    B, n, D = x.shape
    depth = len(weights)
    n_past = past["mla_latent"].shape[2] if past else 0
    n_keys = n_past + n
    key_major = not past
    if past:
        tq, tk = n, _row_tile(n_past, SAMPLE_KEY_TILE)
        deltas, bias_index = _bias_plan(n, n_past, tq, tk, n_past)
        bias_new = _bias_tiles(rel_table, n, n, [0], False)
    else:
        tq, tk = _row_tile(n, PROMPT_Q_TILE), _row_tile(n, PROMPT_K_TILE)
        deltas, bias_index = _bias_plan(n, n, tq, tk, 0)
    bias = _bias_tiles(rel_table, tq, tk, deltas, key_major)
    tm = _row_tile(n, ROW_TILE)
    pos = n_past + jnp.arange(n, dtype=jnp.int32)
    tabs = _rope_tables(pos)
    zero_carry = jnp.zeros((B, 1, LANES), F32)

    new = {k: [] for k in ("mla_latent", "mla_krope", "fox_k", "fox_v", "fox_logf", "diff_k", "diff_v")}
    y = None
    for li in range(depth):
        w = weights[li]
        q_a, k_a, v_a, lat, krope = _mla_in(x, w, tabs, tm, key_major)
        carry = zero_carry
        if past:
            tp = _row_tile(n_past, PAST_ROW_TILE)
            k_a_p, v_a_p = _mla_past(past["mla_latent"][li], past["mla_krope"][li], w, tp)
            fk_p, carry = _fox_past(past["fox_k"][li].reshape(B, n_past, -1), past["fox_logf"][li], w, tp)
        q_b, k_b, v_b, fk, fv, logf, _ = _fox_in(x, w, carry, tm, key_major)
        q_c, k_c, v_c, dk, dv, gates = _diff_gate_in(x, w, tm, key_major)
        lam_init = 0.8 - 0.6 * math.exp(-0.3 * li)
        if past:
            rows_of = lambda name: past[name][li].reshape(B, n_past, -1)
            o_a = _flash(q_a, k_a_p, v_a_p, k_a, v_a, heads=MLA_HEADS, dv=MLA_V, tk=tk, frame=False,
                         name="mla_flash")
            o_b = _flash(q_b, fk_p, rows_of("fox_v"), k_b, v_b, heads=FOX_HEADS, dv=FOX_DIM, tk=tk, frame=True,
                         name="fox_flash")
            o_c = _diff_flash(q_c, rows_of("diff_k"), rows_of("diff_v"), k_c, v_c, bias, bias_new,
                              w["diff_lambda"], w["diff_g"], tk=tk, lam_init=lam_init, bias_index=bias_index)
        else:
            o_a = _flash_t(q_a, k_a, v_a, heads=MLA_HEADS, dv=MLA_V, tq=tq, tk=tk, frame=False, name="mla_flash_t")
            o_b = _flash_t(q_b, k_b, v_b, heads=FOX_HEADS, dv=FOX_DIM, tq=tq, tk=tk, frame=True, name="fox_flash_t")
            o_c = _diff_flash_t(q_c, k_c, v_c, bias, w["diff_lambda"], w["diff_g"], tq=tq, tk=tk,
                                lam_init=lam_init, bias_index=bias_index)
        M = B * n
        flat = lambda a: a.reshape(M, a.shape[-1])
        x1 = _mix_out(flat(x), flat(o_a), flat(o_b), flat(o_c), flat(gates), w, _row_tile(M, MIX_ROW_TILE))
        x2, y = _ffn(x1, w, final_g if li == depth - 1 else None, _row_tile(M, ROW_TILE))
        x = x2.reshape(B, n, D)
        for name, val in (("mla_latent", lat), ("mla_krope", krope), ("fox_k", fk), ("fox_v", fv),
                          ("fox_logf", logf), ("diff_k", dk), ("diff_v", dv)):
            new[name].append(val)
    rows = {
        "mla_latent": jnp.stack(new["mla_latent"]),
        "mla_krope": jnp.stack(new["mla_krope"]),
        "fox_k": jnp.stack(new["fox_k"]).reshape(depth, B, n, FOX_HEADS, FOX_DIM),
        "fox_v": jnp.stack(new["fox_v"]).reshape(depth, B, n, FOX_HEADS, FOX_DIM),
        "fox_logf": jnp.stack(new["fox_logf"]),
        "diff_k": jnp.stack(new["diff_k"]).reshape(depth, B, n, DIFF_HEADS, 2 * DIFF_QK),
        "diff_v": jnp.stack(new["diff_v"]).reshape(depth, B, n, DIFF_HEADS, DIFF_V),
    }
    return y.reshape(B, n, D), rows


def kernel(x_prompt, x_sample, cache_mla_latent, cache_mla_krope, cache_fox_k, cache_fox_v, cache_fox_logf,
           cache_diff_k, cache_diff_v, attn_norm_g, w_in, mla_q_norm_g, mla_kv_norm_g, w_mla_uq, w_mla_ukv,
           fox_f_bias, diff_lambda, diff_subln_g, w_branch, w_o, ffn_norm_g, w_ffn_in, w_ffn_out,
           rel_bias_table, final_norm_g):
    params = {
        "attn_norm_g": attn_norm_g, "w_in": w_in, "mla_q_norm_g": mla_q_norm_g,
        "mla_kv_norm_g": mla_kv_norm_g, "w_mla_uq": w_mla_uq, "w_mla_ukv": w_mla_ukv,
        "fox_f_bias": fox_f_bias, "diff_lambda": diff_lambda, "diff_subln_g": diff_subln_g,
        "w_branch": w_branch, "w_o": w_o, "ffn_norm_g": ffn_norm_g, "w_ffn_in": w_ffn_in,
        "w_ffn_out": w_ffn_out,
    }
    past = {
        "mla_latent": cache_mla_latent, "mla_krope": cache_mla_krope, "fox_k": cache_fox_k,
        "fox_v": cache_fox_v, "fox_logf": cache_fox_logf, "diff_k": cache_diff_k, "diff_v": cache_diff_v,
    }
    depth = w_in.shape[0]
    weights = [_prep_layer_weights(li, params) for li in range(depth)]
    final_g = final_norm_g.reshape(1, -1)
    y_p, sp = _run_trunk(x_prompt, {}, weights, rel_bias_table, final_g)
    y_s, ss = _run_trunk(x_sample, past, weights, rel_bias_table, final_g)
    return (y_p, y_s,
            sp["mla_latent"], ss["mla_latent"], sp["mla_krope"], ss["mla_krope"],
            sp["fox_k"], ss["fox_k"], sp["fox_v"], ss["fox_v"],
            sp["fox_logf"], ss["fox_logf"], sp["diff_k"], ss["diff_k"],
            sp["diff_v"], ss["diff_v"])
```

```python
import functools
import math

import jax
import jax.numpy as jnp
from jax import lax
from jax.experimental import pallas as pl
from jax.experimental.pallas import tpu as pltpu

F32 = jnp.float32
BF16 = jnp.bfloat16

CHUNK_SHIFT = 6
NORM_EPS = 1e-6
LOG2E = math.log2(math.e)
NEG_INF = -1e30
MLA_HEADS, MLA_NOPE, MLA_ROPE, MLA_V = 8, 64, 32, 64
MLA_Q_RANK, MLA_KV_RANK = 384, 256
ROPE_BASE = 10000.0
FOX_HEADS, FOX_DIM = 8, 64
DIFF_HEADS, DIFF_QK, DIFF_V = 4, 64, 128
MIX_WIDTH = 512
N_BRANCHES = 3
REL_BUCKETS, REL_MAX_DIST = 32, 128

LANES = 128
VMEM_LIMIT = 56 * 1024 * 1024


def _cparams(sem):
    return pltpu.CompilerParams(dimension_semantics=sem, vmem_limit_bytes=VMEM_LIMIT)


def _rms(x, g):
    return x * lax.rsqrt(jnp.mean(x * x, axis=-1, keepdims=True) + NORM_EPS) * g


def _dot(a, b):
    return jnp.dot(a, b, preferred_element_type=F32)


def _dot_nt(a, b):
    return lax.dot_general(a, b, (((1,), (1,)), ((), ())), preferred_element_type=F32)


def _bf16_part(x):
    return x.astype(BF16).astype(F32)


def _store_v(v_out, v, transposed):
    v_out[...] = (jnp.transpose(v) if transposed else v).astype(BF16)


def _cumsum_rows(x):
    rows = x.shape[0]
    row = lax.broadcasted_iota(jnp.int32, x.shape, 0)
    shift = 1
    while shift < rows:
        x = x + jnp.where(row >= shift, pltpu.roll(x, shift, 0), 0.0)
        shift *= 2
    return x


def _mla_in_kernel(x_ref, ga_ref, win_ref, gq_ref, wuq_ref, gkv_ref, wukv_ref,
                   cq_ref, sq_ref, ck_ref, sk_ref, lat_stack, kr_stack,
                   q_out, k_out, v_out, lat_out, kr_out, *, v_transposed):
    del lat_stack, kr_stack
    hb = _rms(x_ref[...], ga_ref[...]).astype(BF16)
    pm = _dot(hb, win_ref[...])
    cqn = _rms(pm[:, :MLA_Q_RANK], gq_ref[...]).astype(BF16)
    q2 = _dot(cqn, wuq_ref[...])
    hw = MLA_HEADS * LANES
    cosq = jnp.concatenate([cq_ref[...]] * MLA_HEADS, axis=1)
    sinq = jnp.concatenate([sq_ref[...]] * MLA_HEADS, axis=1)
    q_out[...] = (q2[:, :hw] * cosq + q2[:, hw:] * sinq).astype(BF16)
    latent = _rms(pm[:, MLA_Q_RANK:MLA_Q_RANK + MLA_KV_RANK], gkv_ref[...])
    lat_out[...] = latent
    kv2 = _dot(latent.astype(BF16), wukv_ref[...])
    o = MLA_Q_RANK + MLA_KV_RANK
    krb = pm[:, o:o + LANES] * ck_ref[...] + pm[:, o + LANES:o + 2 * LANES] * sk_ref[...]
    kr_out[...] = krb[:, MLA_NOPE:MLA_NOPE + MLA_ROPE]
    k_out[...] = (kv2[:, :hw] + jnp.concatenate([krb] * MLA_HEADS, axis=1)).astype(BF16)
    _store_v(v_out, kv2[:, hw:], v_transposed)


def _decay_columns(c, rows):
    hi = _bf16_part(c)
    r1 = c - hi
    mid = _bf16_part(r1)
    lo = _bf16_part(r1 - mid)
    lane = lax.broadcasted_iota(jnp.int32, (rows, LANES), 1)
    d = FOX_DIM
    ones_q = jnp.where((lane >= d + 3) & (lane < d + 6), 1.0, 0.0)
    ones_k = jnp.where((lane >= d) & (lane < d + 3), 1.0, 0.0)
    eq, ek = [], []
    for h in range(FOX_HEADS):
        bh = jnp.broadcast_to(hi[:, h:h + 1], (rows, LANES))
        bm = jnp.broadcast_to(mid[:, h:h + 1], (rows, LANES))
        bl = jnp.broadcast_to(lo[:, h:h + 1], (rows, LANES))
        eq.append(jnp.where(lane == d, bh, jnp.where(lane == d + 1, bm, jnp.where(lane == d + 2, bl, ones_q))))
        ek.append(jnp.where(lane == d + 3, -bh, jnp.where(lane == d + 4, -bm, jnp.where(lane == d + 5, -bl, ones_k))))
    return jnp.concatenate(eq, axis=1), jnp.concatenate(ek, axis=1)


def _store_heads(out_ref, x, heads):
    width = x.shape[1] // heads
    for h in range(heads):
        out_ref[:, h, :] = x[:, h * width:(h + 1) * width]


def _fox_in_kernel(x_ref, ga_ref, win_ref, fb_ref, sel_ref, c0_ref, kf_stack, vf_stack, lf_stack,
                   q_out, k_out, vb_out, ct_out, kf_out, vf_out, lf_out, carry_sc, *, v_transposed):
    del kf_stack, vf_stack, lf_stack
    rows = x_ref.shape[0]
    t = pl.program_id(1)

    @pl.when(t == 0)
    def _():
        carry_sc[...] = c0_ref[...]

    hb = _rms(x_ref[...], ga_ref[...]).astype(BF16)
    pf = _dot(hb, win_ref[...])
    w = FOX_HEADS * FOX_DIM
    fq = pf[:, :w] * (FOX_DIM ** -0.5 * LOG2E)
    fk = pf[:, w:2 * w]
    fv = pf[:, 2 * w:3 * w]
    z = pf[:, 3 * w:3 * w + LANES] + fb_ref[...]
    lane = lax.broadcasted_iota(jnp.int32, (rows, LANES), 1)
    logf = jnp.minimum(z, 0.0) - jnp.log1p(jnp.exp(-jnp.abs(z)))
    logf = jnp.where(lane < FOX_HEADS, logf, 0.0)
    lf_out[...] = logf[:, :FOX_HEADS]
    c = carry_sc[...] + _cumsum_rows(logf)
    carry_sc[...] = c[rows - 1:rows, :]
    ct_out[...] = c[rows - 1:rows, :]
    _store_heads(kf_out, fk, FOX_HEADS)
    _store_heads(vf_out, fv, FOX_HEADS)
    _store_v(vb_out, fv, v_transposed)
    eq, ek = _decay_columns(c * LOG2E, rows)
    q_out[...] = (_dot(fq.astype(BF16), sel_ref[...]) + eq).astype(BF16)
    k_out[...] = (_dot(fk.astype(BF16), sel_ref[...]) + ek).astype(BF16)


def _diff_gate_in_kernel(x_ref, ga_ref, wd_ref, wg_ref, dk_stack, dv_stack,
                         dq_out, dkb_out, dvb_out, gate_out, dk_out, dv_out, *, v_transposed):
    del dk_stack, dv_stack
    hb = _rms(x_ref[...], ga_ref[...]).astype(BF16)
    pd = _dot(hb, wd_ref[...])
    w = DIFF_HEADS * 2 * DIFF_QK
    dq_out[...] = (pd[:, :w] * (DIFF_QK ** -0.5 * LOG2E)).astype(BF16)
    dk = pd[:, w:2 * w]
    dv = pd[:, 2 * w:]
    _store_heads(dk_out, dk, DIFF_HEADS)
    _store_heads(dv_out, dv, DIFF_HEADS)
    dkb_out[...] = dk.astype(BF16)
    _store_v(dvb_out, dv, v_transposed)
    gate_out[...] = jax.nn.sigmoid(_dot(hb, wg_ref[...]))


def _mla_past_kernel(lat_ref, kr_ref, wukv_ref, selk_ref, k_out, v_out):
    kv2 = _dot(lat_ref[...].astype(BF16), wukv_ref[...])
    krb = _dot(kr_ref[...].astype(BF16), selk_ref[...])
    hw = MLA_HEADS * LANES
    k_out[...] = (kv2[:, :hw] + jnp.concatenate([krb] * MLA_HEADS, axis=1)).astype(BF16)
    v_out[...] = kv2[:, hw:].astype(BF16)


def _fox_past_kernel(k_ref, lf_ref, sel_ref, k_out, ct_out, carry_sc, lf_sc):
    rows = k_ref.shape[0]
    t = pl.program_id(1)

    @pl.when(t == 0)
    def _():
        carry_sc[...] = jnp.zeros(carry_sc.shape, F32)

    lf_sc[...] = jnp.zeros(lf_sc.shape, F32)
    lf_sc[:, :FOX_HEADS] = lf_ref[...]
    c = carry_sc[...] + _cumsum_rows(lf_sc[...])
    carry_sc[...] = c[rows - 1:rows, :]
    ct_out[...] = c[rows - 1:rows, :]
    _, ek = _decay_columns(c * LOG2E, rows)
    sel = sel_ref[...]
    for h in range(FOX_HEADS):
        blk = _dot(k_ref[:, h, :].astype(BF16), sel) + ek[:, h * LANES:(h + 1) * LANES]
        k_out[:, h * LANES:(h + 1) * LANES] = blk.astype(BF16)


def _block_visibility(q0, k0, tq, tk, frame):
    if frame:
        return k0 <= q0 + (tq - 1), k0 + (tk - 1) <= q0
    return ((k0 >> CHUNK_SHIFT) <= ((q0 + (tq - 1)) >> CHUNK_SHIFT),
            ((k0 + (tk - 1)) >> CHUNK_SHIFT) <= (q0 >> CHUNK_SHIFT))


def _mask(q0, k0, tq, tk, frame):
    row = q0 + lax.broadcasted_iota(jnp.int32, (tq, tk), 0)
    col = k0 + lax.broadcasted_iota(jnp.int32, (tq, tk), 1)
    if frame:
        return col <= row
    return (col >> CHUNK_SHIFT) <= (row >> CHUNK_SHIFT)


def _online_softmax_step(s, v, idx, m_sc, l_sc, acc_sc):
    dv = v.shape[1]
    m_prev = m_sc[idx]
    m_new = jnp.maximum(m_prev, jnp.max(s, axis=1, keepdims=True))
    alpha = jnp.exp2(m_prev - m_new)
    p = jnp.exp2(s - m_new[:, :1])
    l_sc[idx] = alpha * l_sc[idx] + jnp.sum(p, axis=1, keepdims=True)
    m_sc[idx] = m_new
    acc_sc[idx] = acc_sc[idx] * alpha[:, :dv] + _dot(p.astype(BF16), v)


def _head_cols(ref, h, width):
    if len(ref.shape) == 3:
        return ref[:, h, :]
    return ref[:, h * width:(h + 1) * width]


def _init_softmax_state(m_sc, l_sc, acc_sc):
    m_sc[...] = jnp.full(m_sc.shape, NEG_INF, F32)
    l_sc[...] = jnp.zeros(l_sc.shape, F32)
    acc_sc[...] = jnp.zeros(acc_sc.shape, F32)


def _flash_kernel(q_ref, kp_ref, vp_ref, kn_ref, vn_ref, o_ref, m_sc, l_sc, acc_sc, *, heads, dv, n_past, frame):
    kj = pl.program_id(1)
    n_hist = pl.num_programs(1) - 1
    tq = q_ref.shape[0]

    @pl.when(kj == 0)
    def _():
        _init_softmax_state(m_sc, l_sc, acc_sc)

    def step(k_ref, v_ref, mask):
        scores = lambda h: _dot_nt(q_ref[:, h * LANES:(h + 1) * LANES],
                                   k_ref[:, h * LANES:(h + 1) * LANES].astype(BF16))
        s_next = scores(0)
        for h in range(heads):
            s = s_next
            if h + 1 < heads:
                s_next = scores(h + 1)
            if mask is not None:
                s = jnp.where(mask, s, NEG_INF)
            _online_softmax_step(s, _head_cols(v_ref, h, dv).astype(BF16), h, m_sc, l_sc, acc_sc)

    @pl.when(kj < n_hist)
    def _():
        step(kp_ref, vp_ref, None)

    @pl.when(kj == n_hist)
    def _():
        step(kn_ref, vn_ref, _mask(n_past, n_past, tq, kn_ref.shape[0], frame))
        for h in range(heads):
            o_ref[:, h * dv:(h + 1) * dv] = (acc_sc[h] / l_sc[h][:, :dv]).astype(o_ref.dtype)


def _diff_flash_kernel(lam_ref, g_ref, q_ref, kp_ref, vp_ref, kn_ref, vn_ref, bp_ref, bn_ref, o_ref,
                       m_sc, l_sc, acc_sc, *, n_past, lam_init):
    kj = pl.program_id(1)
    n_hist = pl.num_programs(1) - 1
    tq = q_ref.shape[0]

    @pl.when(kj == 0)
    def _():
        _init_softmax_state(m_sc, l_sc, acc_sc)

    def step(k_ref, v_ref, b_ref, mask):
        lane = lax.broadcasted_iota(jnp.int32, (tq, LANES), 1)

        def scores(i):
            h, c = divmod(i, 2)
            q = q_ref[:, h * LANES:(h + 1) * LANES]
            qc = jnp.where((lane < DIFF_QK) if c == 0 else (lane >= DIFF_QK), q, jnp.zeros_like(q))
            return _dot_nt(qc, _head_cols(k_ref, h, LANES).astype(BF16))

        n_maps = 2 * DIFF_HEADS
        s_next = scores(0)
        for i in range(n_maps):
            s = s_next + b_ref[i // 2]
            if i + 1 < n_maps:
                s_next = scores(i + 1)
            if mask is not None:
                s = jnp.where(mask, s, NEG_INF)
            v = _head_cols(v_ref, i // 2, LANES).astype(BF16)
            _online_softmax_step(s, v, i, m_sc, l_sc, acc_sc)

    @pl.when(kj < n_hist)
    def _():
        step(kp_ref, vp_ref, bp_ref, None)

    @pl.when(kj == n_hist)
    def _():
        step(kn_ref, vn_ref, bn_ref, _mask(n_past, n_past, tq, kn_ref.shape[0], False))
        lp = lam_ref[...]
        lam = (jnp.exp(jnp.sum(lp[0:1, :] * lp[1:2, :], axis=1, keepdims=True))
               - jnp.exp(jnp.sum(lp[2:3, :] * lp[3:4, :], axis=1, keepdims=True)) + lam_init)
        g = g_ref[...]
        for h in range(DIFF_HEADS):
            o = acc_sc[2 * h] / l_sc[2 * h] - lam * (acc_sc[2 * h + 1] / l_sc[2 * h + 1])
            o_ref[:, h * LANES:(h + 1) * LANES] = (_rms(o, g) * (1.0 - lam_init)).astype(o_ref.dtype)


def _mask_t(q0, k0, tq, tk, frame):
    key = k0 + lax.broadcasted_iota(jnp.int32, (tk, tq), 0)
    qry = q0 + lax.broadcasted_iota(jnp.int32, (tk, tq), 1)
    if frame:
        return key <= qry
    return (key >> CHUNK_SHIFT) <= (qry >> CHUNK_SHIFT)


def _softmax_weights_t(st, idx, m_sc):
    m_prev = m_sc[idx]
    m_new = jnp.maximum(m_prev, jnp.max(st, axis=0, keepdims=True))
    m_sc[idx] = m_new
    return jnp.exp2(st - m_new).astype(BF16), jnp.exp2(m_prev - m_new)


def _accumulate_t(p, alpha, vt, idx, acc_sc):
    acc_sc[idx] = acc_sc[idx] * alpha + _dot(vt, p)


def _pipelined_maps_t(n_maps, scores, logits, values, m_sc, acc_sc):
    st_next = scores(0)
    pending = None
    for i in range(n_maps + 1):
        st = st_next
        if i + 1 < n_maps:
            st_next = scores(i + 1)
        current = (_softmax_weights_t(logits(i, st), i, m_sc) + (i,)) if i < n_maps else None
        if pending is not None:
            p, alpha, j = pending
            _accumulate_t(p, alpha, values(j), j, acc_sc)
        pending = current


def _values_with_ones(vt_ref, h, dv, tk):
    return jnp.concatenate([vt_ref[h * dv:(h + 1) * dv, :], jnp.ones((BF16_SUBLANES, tk), BF16)], axis=0)


def _normalised(acc, dv):
    return acc[:dv, :] / acc[dv:dv + 1, :]


def _flash_t_kernel(q_ref, k_ref, vt_ref, o_ref, m_sc, acc_sc, *, heads, dv, tq, tk, cw, frame):
    qi, kj = pl.program_id(1), pl.program_id(2)
    q0 = qi * tq
    k0 = kj * tk
    n_col = tq // cw

    @pl.when(kj == 0)
    def _():
        m_sc[...] = jnp.full(m_sc.shape, NEG_INF, F32)
        acc_sc[...] = jnp.zeros(acc_sc.shape, F32)

    needed, full = _block_visibility(q0, k0, tq, tk, frame)

    def step(masked):
        mask = _mask_t(q0, k0, tq, tk, frame) if masked else None

        def scores(i):
            h, c = divmod(i, n_col)
            return _dot_nt(k_ref[:, h * LANES:(h + 1) * LANES], q_ref[c * cw:(c + 1) * cw, h * LANES:(h + 1) * LANES])

        def logits(i, st):
            c = i % n_col
            return jnp.where(mask[:, c * cw:(c + 1) * cw], st, NEG_INF) if masked else st

        values = lambda i: _values_with_ones(vt_ref, i // n_col, dv, tk)
        _pipelined_maps_t(heads * n_col, scores, logits, values, m_sc, acc_sc)

    @pl.when(needed & full)
    def _():
        step(False)

    @pl.when(needed & jnp.logical_not(full))
    def _():
        step(True)

    @pl.when(kj == pl.num_programs(2) - 1)
    def _():
        per_blk = LANES // dv
        for g in range(heads // per_blk):
            for c in range(n_col):
                rows = [_normalised(acc_sc[(g * per_blk + i) * n_col + c], dv) for i in range(per_blk)]
                blk = rows[0] if per_blk == 1 else jnp.concatenate(rows, axis=0)
                o_ref[c * cw:(c + 1) * cw, g * LANES:(g + 1) * LANES] = jnp.transpose(blk).astype(o_ref.dtype)


def _diff_flash_t_kernel(lam_ref, g_ref, q_ref, k_ref, vt_ref, b_ref, o_ref, m_sc, acc_sc,
                         *, tq, tk, lam_init):
    qi, kj = pl.program_id(1), pl.program_id(2)
    q0 = qi * tq
    k0 = kj * tk

    @pl.when(kj == 0)
    def _():
        m_sc[...] = jnp.full(m_sc.shape, NEG_INF, F32)
        acc_sc[...] = jnp.zeros(acc_sc.shape, F32)

    needed, full = _block_visibility(q0, k0, tq, tk, False)

    def step(masked):
        mask = _mask_t(q0, k0, tq, tk, False) if masked else None
        lane = lax.broadcasted_iota(jnp.int32, (tq, LANES), 1)

        def scores(i):
            h, c = divmod(i, 2)
            q = q_ref[:, h * LANES:(h + 1) * LANES]
            qc = jnp.where((lane < DIFF_QK) if c == 0 else (lane >= DIFF_QK), q, jnp.zeros_like(q))
            return _dot_nt(k_ref[:, h * LANES:(h + 1) * LANES], qc)

        def logits(i, st):
            st = st + b_ref[i // 2]
            return jnp.where(mask, st, NEG_INF) if masked else st

        values = lambda i: _values_with_ones(vt_ref, i // 2, DIFF_V, tk)
        _pipelined_maps_t(2 * DIFF_HEADS, scores, logits, values, m_sc, acc_sc)

    @pl.when(needed & full)
    def _():
        step(False)

    @pl.when(needed & jnp.logical_not(full))
    def _():
        step(True)

    @pl.when(kj == pl.num_programs(2) - 1)
    def _():
        lp = lam_ref[...]
        lam = (jnp.exp(jnp.sum(lp[0:1, :] * lp[1:2, :], axis=1, keepdims=True))
               - jnp.exp(jnp.sum(lp[2:3, :] * lp[3:4, :], axis=1, keepdims=True)) + lam_init)
        g = g_ref[...]
        for h in range(DIFF_HEADS):
            o_t = _normalised(acc_sc[2 * h], DIFF_V) - lam * _normalised(acc_sc[2 * h + 1], DIFF_V)
            o = jnp.transpose(o_t)
            o_ref[:, h * LANES:(h + 1) * LANES] = (_rms(o, g) * (1.0 - lam_init)).astype(o_ref.dtype)


def _mix_out_kernel(x_ref, oa_ref, ob_ref, oc_ref, gate_ref, wb_ref, wo_ref, x_out):
    d = x_ref.shape[1]
    mix = None
    for j, o_ref in enumerate((oa_ref, ob_ref, oc_ref)):
        term = gate_ref[:, j * d:(j + 1) * d] * _dot(o_ref[...], wb_ref[j])
        mix = term if mix is None else mix + term
    x_out[...] = x_ref[...] + _dot(mix.astype(BF16), wo_ref[...])


def _ffn_kernel(x_ref, g_ref, win_ref, wout_ref, gfin_ref, x_out, y_out=None):
    x = x_ref[...]
    gu = _dot(_rms(x, g_ref[...]).astype(BF16), win_ref[...])
    dff = wout_ref.shape[0]
    gate, up = gu[:, :dff], gu[:, dff:]
    x2 = x + _dot((gate * jax.nn.sigmoid(gate) * up).astype(BF16), wout_ref[...])
    x_out[...] = x2
    if y_out is not None:
        y_out[...] = _rms(x2, gfin_ref[...])


BF16_SUBLANES = 16
ROW_TILE = 256
MIX_ROW_TILE = 512
PAST_ROW_TILE = 512
PROMPT_Q_TILE = 1024
PROMPT_K_TILE = 512
PROMPT_Q_COLS = 1024
SAMPLE_KEY_TILE = 2048


def _row_tile(n, target):
    for t in range(min(n, target), 0, -1):
        if n % t == 0 and (t % BF16_SUBLANES == 0 or t == n):
            return t
    raise ValueError((n, target))


def _full_spec(arr):
    nd = arr.ndim
    return pl.BlockSpec(arr.shape, lambda *_: (0,) * nd)


def _rows3(width, tm):
    return pl.BlockSpec((None, tm, width), lambda b, t: (b, t, 0))


def _layer_rows_spec(arr, li, tm):
    tail = arr.shape[3:]
    return pl.BlockSpec((None, None, tm) + tail, lambda b, t: (li, b, t) + (0,) * len(tail))


ALIASED = pl.BlockSpec(memory_space=pl.ANY)


def _work_specs(outs, B, n, tm, v_index, v_t):
    specs = [_rows3(wd, tm) for wd, _ in outs]
    shapes = [jax.ShapeDtypeStruct((B, n, wd), dt) for wd, dt in outs]
    if v_t:
        wd, dt = outs[v_index]
        specs[v_index] = pl.BlockSpec((None, wd, tm), lambda b, t: (b, 0, t))
        shapes[v_index] = jax.ShapeDtypeStruct((B, wd, n), dt)
    return specs, shapes


def _stacked_outputs(stacks, li, tm, n_inputs, n_work_outputs):
    specs = [_layer_rows_spec(a, li, tm) for a in stacks]
    shapes = [jax.ShapeDtypeStruct(a.shape, a.dtype) for a in stacks]
    aliases = {n_inputs + i: n_work_outputs + i for i in range(len(stacks))}
    return specs, shapes, aliases


def _mla_in(x, w, tabs, stacks, li, tm, v_t):
    B, n, D = x.shape
    hw = MLA_HEADS * LANES
    tab_spec = pl.BlockSpec((tm, LANES), lambda b, t: (t, 0))
    consts = (w["attn_g"], w["mla_in"], w["q_g"], w["uq2"], w["kv_g"], w["ukv2"])
    outs = [(hw, BF16), (hw, BF16), (MLA_HEADS * MLA_V, BF16)]
    specs, shapes = _work_specs(outs, B, n, tm, 2, v_t)
    n_in = 1 + len(consts) + len(tabs)
    s_specs, s_shapes, aliases = _stacked_outputs(stacks, li, tm, n_in, len(outs))
    return pl.pallas_call(
        functools.partial(_mla_in_kernel, v_transposed=v_t),
        grid=(B, n // tm),
        in_specs=[_rows3(D, tm)] + [_full_spec(a) for a in consts] + [tab_spec] * 4 + [ALIASED] * len(stacks),
        out_specs=specs + s_specs,
        out_shape=shapes + s_shapes,
        input_output_aliases=aliases,
        compiler_params=_cparams(("parallel", "parallel")),
        name="mla_in",
    )(x, *consts, *tabs, *stacks)


def _fox_in(x, w, c0, stacks, li, tm, v_t):
    B, n, D = x.shape
    hw = FOX_HEADS * LANES
    wd = FOX_HEADS * FOX_DIM
    consts = (w["attn_g"], w["fox_in"], w["fox_bias"], w["fox_sel"])
    row1 = pl.BlockSpec((None, 1, LANES), lambda b, t: (b, 0, 0))
    outs = [(hw, BF16), (hw, BF16), (wd, BF16)]
    specs, shapes = _work_specs(outs, B, n, tm, 2, v_t)
    n_in = 1 + len(consts) + 1
    s_specs, s_shapes, aliases = _stacked_outputs(stacks, li, tm, n_in, len(outs) + 1)
    return pl.pallas_call(
        functools.partial(_fox_in_kernel, v_transposed=v_t),
        grid=(B, n // tm),
        in_specs=[_rows3(D, tm)] + [_full_spec(a) for a in consts] + [row1] + [ALIASED] * len(stacks),
        out_specs=specs + [row1] + s_specs,
        out_shape=shapes + [jax.ShapeDtypeStruct((B, 1, LANES), F32)] + s_shapes,
        input_output_aliases=aliases,
        scratch_shapes=[pltpu.VMEM((1, LANES), F32)],
        compiler_params=_cparams(("arbitrary", "arbitrary")),
        name="fox_in",
    )(x, *consts, c0, *stacks)


def _diff_gate_in(x, w, stacks, li, tm, v_t):
    B, n, D = x.shape
    wd = DIFF_HEADS * DIFF_V
    consts = (w["attn_g"], w["diff_in"], w["gate_in"])
    outs = [(wd, BF16), (wd, BF16), (wd, BF16), (N_BRANCHES * D, F32)]
    specs, shapes = _work_specs(outs, B, n, tm, 2, v_t)
    n_in = 1 + len(consts)
    s_specs, s_shapes, aliases = _stacked_outputs(stacks, li, tm, n_in, len(outs))
    return pl.pallas_call(
        functools.partial(_diff_gate_in_kernel, v_transposed=v_t),
        grid=(B, n // tm),
        in_specs=[_rows3(D, tm)] + [_full_spec(a) for a in consts] + [ALIASED] * len(stacks),
        out_specs=specs + s_specs,
        out_shape=shapes + s_shapes,
        input_output_aliases=aliases,
        compiler_params=_cparams(("parallel", "parallel")),
        name="diff_gate_in",
    )(x, *consts, *stacks)


def _mla_past(lat, kr, li, w, tm):
    _, B, n, _ = lat.shape
    hw = MLA_HEADS * LANES
    consts = (w["ukv2"], w["krope_sel"])
    return pl.pallas_call(
        _mla_past_kernel,
        grid=(B, n // tm),
        in_specs=[_layer_rows_spec(lat, li, tm), _layer_rows_spec(kr, li, tm)] + [_full_spec(a) for a in consts],
        out_specs=[_rows3(hw, tm), _rows3(MLA_HEADS * MLA_V, tm)],
        out_shape=[jax.ShapeDtypeStruct((B, n, hw), BF16), jax.ShapeDtypeStruct((B, n, MLA_HEADS * MLA_V), BF16)],
        compiler_params=_cparams(("parallel", "parallel")),
        name="mla_past",
    )(lat, kr, *consts)


def _fox_past(k, lf, li, w, tm):
    _, B, n, _, _ = k.shape
    hw = FOX_HEADS * LANES
    row1 = pl.BlockSpec((None, 1, LANES), lambda b, t: (b, 0, 0))
    return pl.pallas_call(
        _fox_past_kernel,
        grid=(B, n // tm),
        in_specs=[_layer_rows_spec(k, li, tm), _layer_rows_spec(lf, li, tm), _full_spec(w["fox_sel1"])],
        out_specs=[_rows3(hw, tm), row1],
        out_shape=[jax.ShapeDtypeStruct((B, n, hw), BF16), jax.ShapeDtypeStruct((B, 1, LANES), F32)],
        scratch_shapes=[pltpu.VMEM((1, LANES), F32), pltpu.VMEM((tm, LANES), F32)],
        compiler_params=_cparams(("arbitrary", "arbitrary")),
        name="fox_past",
    )(k, lf, w["fox_sel1"])


def _last_needed_block(qi, tq, tk, q_off, nk, frame):
    last_pos = q_off + qi * tq + (tq - 1)
    if not frame:
        last_pos = ((last_pos >> CHUNK_SHIFT) << CHUNK_SHIFT) + ((1 << CHUNK_SHIFT) - 1)
    return jnp.minimum(last_pos // tk, nk - 1)


def _hist_spec(arr, tk, n_hist, li):
    if arr.ndim == 3:
        return pl.BlockSpec((None, tk, arr.shape[2]), lambda b, kj: (b, jnp.minimum(kj, n_hist - 1), 0))
    tail = arr.shape[3:]
    return pl.BlockSpec((None, None, tk) + tail,
                        lambda b, kj: (li, b, jnp.minimum(kj, n_hist - 1)) + (0,) * len(tail))


def _hist_rows(arr):
    return arr.shape[1] if arr.ndim == 3 else arr.shape[2]


def _per_batch_spec(arr):
    return pl.BlockSpec((None,) + arr.shape[1:], lambda b, kj: (b, 0, 0))


def _flash(q, k_past, v_past, k_new, v_new, *, li, heads, dv, tk, frame, name):
    B, n, _ = q.shape
    n_past = _hist_rows(k_past)
    n_hist = n_past // tk
    kern = functools.partial(_flash_kernel, heads=heads, dv=dv, n_past=n_past, frame=frame)
    return pl.pallas_call(
        kern,
        grid=(B, n_hist + 1),
        in_specs=[_per_batch_spec(q), _hist_spec(k_past, tk, n_hist, li), _hist_spec(v_past, tk, n_hist, li),
                  _per_batch_spec(k_new), _per_batch_spec(v_new)],
        out_specs=pl.BlockSpec((None, n, heads * dv), lambda b, kj: (b, 0, 0)),
        out_shape=jax.ShapeDtypeStruct((B, n, heads * dv), BF16),
        scratch_shapes=[pltpu.VMEM((heads, n, LANES), F32), pltpu.VMEM((heads, n, LANES), F32),
                        pltpu.VMEM((heads, n, dv), F32)],
        compiler_params=_cparams(("parallel", "arbitrary")),
        name=name,
    )(q, k_past, v_past, k_new, v_new)


def _diff_flash(q, k_past, v_past, k_new, v_new, bias_past, bias_new, lam_p, g_sub, *, li, tk, lam_init,
                bias_index):
    B, n, w = q.shape
    n_past = _hist_rows(k_past)
    n_hist = n_past // tk
    kern = functools.partial(_diff_flash_kernel, n_past=n_past, lam_init=lam_init)
    return pl.pallas_call(
        kern,
        grid=(B, n_hist + 1),
        in_specs=[_full_spec(lam_p), _full_spec(g_sub), _per_batch_spec(q),
                  _hist_spec(k_past, tk, n_hist, li), _hist_spec(v_past, tk, n_hist, li),
                  _per_batch_spec(k_new), _per_batch_spec(v_new),
                  pl.BlockSpec((None, DIFF_HEADS, n, tk), lambda b, kj: (bias_index(0, kj), 0, 0, 0)),
                  pl.BlockSpec((None, DIFF_HEADS, n, n), lambda b, kj: (0, 0, 0, 0))],
        out_specs=pl.BlockSpec((None, n, w), lambda b, kj: (b, 0, 0)),
        out_shape=jax.ShapeDtypeStruct((B, n, w), BF16),
        scratch_shapes=[pltpu.VMEM((2 * DIFF_HEADS, n, LANES), F32), pltpu.VMEM((2 * DIFF_HEADS, n, LANES), F32),
                        pltpu.VMEM((2 * DIFF_HEADS, n, DIFF_V), F32)],
        compiler_params=_cparams(("parallel", "arbitrary")),
        name="diff_flash",
    )(lam_p, g_sub, q, k_past, v_past, k_new, v_new, bias_past, bias_new)


def _flash_t(q, k, vt, *, heads, dv, tq, tk, frame, name):
    B, n, qw = q.shape
    nq, nk = n // tq, n // tk

    def last(qi):
        return _last_needed_block(qi, tq, tk, 0, nk, frame)

    cw = min(tq, PROMPT_Q_COLS)
    n_maps = heads * (tq // cw)
    kern = functools.partial(_flash_t_kernel, heads=heads, dv=dv, tq=tq, tk=tk, cw=cw, frame=frame)
    return pl.pallas_call(
        kern,
        grid=(B, nq, nk),
        in_specs=[pl.BlockSpec((None, tq, qw), lambda b, qi, kj: (b, qi, 0)),
                  pl.BlockSpec((None, tk, qw), lambda b, qi, kj: (b, jnp.minimum(kj, last(qi)), 0)),
                  pl.BlockSpec((None, heads * dv, tk), lambda b, qi, kj: (b, 0, jnp.minimum(kj, last(qi))))],
        out_specs=pl.BlockSpec((None, tq, heads * dv), lambda b, qi, kj: (b, qi, 0)),
        out_shape=jax.ShapeDtypeStruct((B, n, heads * dv), BF16),
        scratch_shapes=[pltpu.VMEM((n_maps, 1, cw), F32),
                        pltpu.VMEM((n_maps, dv + BF16_SUBLANES, cw), F32)],
        compiler_params=_cparams(("parallel", "parallel", "arbitrary")),
        name=name,
    )(q, k, vt)


def _diff_flash_t(q, k, vt, bias, lam_p, g_sub, *, tq, tk, lam_init, bias_index):
    B, n, w = q.shape
    nq, nk = n // tq, n // tk

    def last(qi):
        return _last_needed_block(qi, tq, tk, 0, nk, False)

    kern = functools.partial(_diff_flash_t_kernel, tq=tq, tk=tk, lam_init=lam_init)
    return pl.pallas_call(
        kern,
        grid=(B, nq, nk),
        in_specs=[_full_spec(lam_p), _full_spec(g_sub),
                  pl.BlockSpec((None, tq, w), lambda b, qi, kj: (b, qi, 0)),
                  pl.BlockSpec((None, tk, w), lambda b, qi, kj: (b, jnp.minimum(kj, last(qi)), 0)),
                  pl.BlockSpec((None, w, tk), lambda b, qi, kj: (b, 0, jnp.minimum(kj, last(qi)))),
                  pl.BlockSpec((None, DIFF_HEADS, tk, tq), lambda b, qi, kj: (bias_index(qi, kj), 0, 0, 0))],
        out_specs=pl.BlockSpec((None, tq, w), lambda b, qi, kj: (b, qi, 0)),
        out_shape=jax.ShapeDtypeStruct((B, n, w), BF16),
        scratch_shapes=[pltpu.VMEM((2 * DIFF_HEADS, 1, tq), F32),
                        pltpu.VMEM((2 * DIFF_HEADS, DIFF_V + BF16_SUBLANES, tq), F32)],
        compiler_params=_cparams(("parallel", "parallel", "arbitrary")),
        name="diff_flash_t",
    )(lam_p, g_sub, q, k, vt, bias)


def _mix_out(x, oa, ob, oc, gates, w, tm):
    M, D = x.shape
    rows = lambda wd: pl.BlockSpec((tm, wd), lambda t: (t, 0))
    return pl.pallas_call(
        _mix_out_kernel,
        grid=(M // tm,),
        in_specs=[rows(D), rows(MIX_WIDTH), rows(MIX_WIDTH), rows(MIX_WIDTH), rows(N_BRANCHES * D),
                  _full_spec(w["branch"]), _full_spec(w["o"])],
        out_specs=rows(D),
        out_shape=jax.ShapeDtypeStruct((M, D), F32),
        compiler_params=_cparams(("parallel",)),
        name="mix_out",
    )(x, oa, ob, oc, gates, w["branch"], w["o"])


def _ffn(x, w, final_g, tm):
    M, D = x.shape
    rows = pl.BlockSpec((tm, D), lambda t: (t, 0))
    consts = (w["ffn_g"], w["ffn_in"], w["ffn_out"], final_g)
    last = final_g is not None
    if not last:
        consts = consts[:3] + (w["ffn_g"],)
    out_shape = [jax.ShapeDtypeStruct((M, D), F32)] * (2 if last else 1)
    res = pl.pallas_call(
        _ffn_kernel,
        grid=(M // tm,),
        in_specs=[rows] + [_full_spec(a) for a in consts],
        out_specs=[rows] * len(out_shape),
        out_shape=out_shape,
        compiler_params=_cparams(("parallel",)),
        name="ffn_final" if last else "ffn",
    )(x, *consts)
    return res if last else (res[0], None)


def _rot_cols(w):
    half = MLA_ROPE // 2
    return jnp.concatenate([-w[..., half:], w[..., :half]], axis=-1)


def _head_blocks(w, heads, width):
    k = w.shape[0]
    w = w.reshape(k, heads, width)
    return jnp.pad(w, ((0, 0), (0, 0), (0, LANES - width))).reshape(k, heads * LANES)


def _prep_layer_weights(li, p):
    d = p["w_in"].shape[1]
    sizes = (MLA_Q_RANK, MLA_KV_RANK, MLA_ROPE,
             FOX_HEADS * FOX_DIM, FOX_HEADS * FOX_DIM, FOX_HEADS * FOX_DIM, FOX_HEADS,
             DIFF_HEADS * 2 * DIFF_QK, DIFF_HEADS * 2 * DIFF_QK, DIFF_HEADS * DIFF_V, N_BRANCHES * d)
    offs = [0]
    for s in sizes:
        offs.append(offs[-1] + s)
    win = p["w_in"][li]
    col = lambda i: win[:, offs[i]:offs[i + 1]]
    w_cq, w_ckv, w_kr, w_fq, w_fk, w_fv, w_ff, w_dq, w_dk, w_dv, w_gl = (col(i) for i in range(11))

    def rope_block(wr):
        return jnp.pad(wr, ((0, 0), (MLA_NOPE, LANES - MLA_NOPE - MLA_ROPE)))

    mla_in = jnp.concatenate([w_cq, w_ckv, rope_block(w_kr), rope_block(_rot_cols(w_kr))], axis=1)
    uq = p["w_mla_uq"][li].reshape(MLA_Q_RANK, MLA_HEADS, MLA_NOPE + MLA_ROPE)
    uq_nope, uq_rope = uq[..., :MLA_NOPE], uq[..., MLA_NOPE:]
    pad = jnp.zeros((MLA_Q_RANK, MLA_HEADS, LANES - MLA_NOPE - MLA_ROPE), F32)
    uq_cat = jnp.concatenate([uq_nope, uq_rope, pad], axis=-1).reshape(MLA_Q_RANK, MLA_HEADS * LANES)
    uq_rot = jnp.concatenate([jnp.zeros_like(uq_nope), _rot_cols(uq_rope), pad], axis=-1)
    uq2 = jnp.concatenate([uq_cat, uq_rot.reshape(MLA_Q_RANK, MLA_HEADS * LANES)], axis=1)
    ukv = p["w_mla_ukv"][li].reshape(MLA_KV_RANK, MLA_HEADS, MLA_NOPE + MLA_V)
    ukn = _head_blocks(ukv[..., :MLA_NOPE].reshape(MLA_KV_RANK, -1), MLA_HEADS, MLA_NOPE)
    uv = ukv[..., MLA_NOPE:].reshape(MLA_KV_RANK, MLA_HEADS * MLA_V)
    ukv2 = jnp.concatenate([ukn, uv], axis=1)
    fox_in = jnp.concatenate([w_fq, w_fk, w_fv, jnp.pad(w_ff, ((0, 0), (0, LANES - FOX_HEADS)))], axis=1)
    fox_sel = _head_blocks(jnp.eye(FOX_HEADS * FOX_DIM, dtype=F32), FOX_HEADS, FOX_DIM)
    krope_sel = jnp.pad(jnp.eye(MLA_ROPE, dtype=F32), ((0, 0), (MLA_NOPE, LANES - MLA_NOPE - MLA_ROPE)))
    row = lambda a: a.reshape(1, -1)
    return {
        "attn_g": row(p["attn_norm_g"][li]),
        "mla_in": mla_in.astype(BF16),
        "q_g": row(p["mla_q_norm_g"][li]),
        "uq2": uq2.astype(BF16),
        "kv_g": row(p["mla_kv_norm_g"][li]),
        "ukv2": ukv2.astype(BF16),
        "krope_sel": krope_sel.astype(BF16),
        "fox_in": fox_in.astype(BF16),
        "fox_bias": jnp.pad(row(p["fox_f_bias"][li]), ((0, 0), (0, LANES - FOX_HEADS))),
        "fox_sel": fox_sel.astype(BF16),
        "fox_sel1": jnp.pad(jnp.eye(FOX_DIM, dtype=F32), ((0, 0), (0, LANES - FOX_DIM))).astype(BF16),
        "diff_in": jnp.concatenate([w_dq, w_dk, w_dv], axis=1).astype(BF16),
        "gate_in": w_gl.astype(BF16),
        "diff_lambda": p["diff_lambda"][li],
        "diff_g": row(p["diff_subln_g"][li]),
        "branch": p["w_branch"][li].astype(BF16),
        "o": p["w_o"][li].astype(BF16),
        "ffn_g": row(p["ffn_norm_g"][li]),
        "ffn_in": p["w_ffn_in"][li].astype(BF16),
        "ffn_out": p["w_ffn_out"][li].astype(BF16),
    }


def _rope_tables(pos):
    half = MLA_ROPE // 2
    inv = 1.0 / (ROPE_BASE ** (jnp.arange(half, dtype=F32) / half))
    ang = pos.astype(F32)[:, None] * inv[None, :]
    cos, sin = jnp.cos(ang), jnp.sin(ang)
    cos2, sin2 = jnp.concatenate([cos, cos], axis=1), jnp.concatenate([sin, sin], axis=1)
    n = pos.shape[0]
    tail = jnp.zeros((n, LANES - MLA_NOPE - MLA_ROPE), F32)
    scale = (MLA_NOPE + MLA_ROPE) ** -0.5 * LOG2E
    cos_q = jnp.concatenate([jnp.ones((n, MLA_NOPE), F32), cos2, tail], axis=1) * scale
    sin_q = jnp.concatenate([jnp.zeros((n, MLA_NOPE), F32), sin2, tail], axis=1) * scale
    cos_k = jnp.concatenate([jnp.zeros((n, MLA_NOPE), F32), cos2, tail], axis=1)
    sin_k = jnp.concatenate([jnp.zeros((n, MLA_NOPE), F32), sin2, tail], axis=1)
    return cos_q, sin_q, cos_k, sin_k


def _rel_bucket(rel):
    half = REL_BUCKETS // 2
    max_exact = half // 2
    ret = jnp.where(rel > 0, half, 0)
    n = jnp.abs(rel)
    large = max_exact + (jnp.log(jnp.maximum(n, 1).astype(F32) / max_exact)
                         / math.log(REL_MAX_DIST / max_exact) * (half - max_exact)).astype(jnp.int32)
    large = jnp.minimum(large, half - 1)
    return ret + jnp.where(n < max_exact, n, large)


def _bias_tiles(rel_table, tq, tk, deltas, key_major):
    ii = jnp.arange(tq, dtype=jnp.int32)
    jj = jnp.arange(tk, dtype=jnp.int32)
    rel0 = (jj[:, None] - ii[None, :]) if key_major else (jj[None, :] - ii[:, None])
    table = rel_table.astype(F32) * LOG2E
    tiles = []
    for d in deltas:
        bucket = _rel_bucket(rel0 - d)[None]
        tile = jnp.zeros((DIFF_HEADS,) + rel0.shape, F32)
        for b in range(REL_BUCKETS):
            tile = jnp.where(bucket == b, table[b][:, None, None], tile)
        tiles.append(tile)
    return jnp.stack(tiles)


def _bias_plan(n, n_keys, tq, tk, q_off):
    nq, nk = n // tq, n_keys // tk
    step = math.gcd(tq, tk) if nq > 1 else tk
    needed = [q_off + qi * tq - kj * tk for qi in range(nq) for kj in range(nk)
              if ((kj * tk) >> CHUNK_SHIFT) <= ((q_off + qi * tq + tq - 1) >> CHUNK_SHIFT)]
    d_min = min(needed)
    far_from = tk + REL_MAX_DIST - 1
    deltas = list(range(d_min, max(far_from, d_min + 1), step))
    n_near = len(deltas)
    deltas.append(deltas[-1] + step)
    base = (q_off - d_min) // step

    def index(qi, kj):
        return jnp.clip(base + qi * (tq // step) - kj * (tk // step), 0, n_near)

    return deltas, index


def _run_trunk(x, past, weights, rel_table, final_g):
    B, n, D = x.shape
    depth = len(weights)
    n_past = past["mla_latent"].shape[2] if past else 0
    n_keys = n_past + n
    key_major = not past
    if past:
        tq, tk = n, _row_tile(n_past, SAMPLE_KEY_TILE)
        deltas, bias_index = _bias_plan(n, n_past, tq, tk, n_past)
        bias_new = _bias_tiles(rel_table, n, n, [0], False)
    else:
        tq, tk = _row_tile(n, PROMPT_Q_TILE), _row_tile(n, PROMPT_K_TILE)
        deltas, bias_index = _bias_plan(n, n, tq, tk, 0)
    bias = _bias_tiles(rel_table, tq, tk, deltas, key_major)
    tm = _row_tile(n, ROW_TILE)
    pos = n_past + jnp.arange(n, dtype=jnp.int32)
    tabs = _rope_tables(pos)
    zero_carry = jnp.zeros((B, 1, LANES), F32)

    rows = {
        "mla_latent": jnp.zeros((depth, B, n, MLA_KV_RANK), F32),
        "mla_krope": jnp.zeros((depth, B, n, MLA_ROPE), F32),
        "fox_k": jnp.zeros((depth, B, n, FOX_HEADS, FOX_DIM), F32),
        "fox_v": jnp.zeros((depth, B, n, FOX_HEADS, FOX_DIM), F32),
        "fox_logf": jnp.zeros((depth, B, n, FOX_HEADS), F32),
        "diff_k": jnp.zeros((depth, B, n, DIFF_HEADS, 2 * DIFF_QK), F32),
        "diff_v": jnp.zeros((depth, B, n, DIFF_HEADS, DIFF_V), F32),
    }
    y = None
    for li in range(depth):
        w = weights[li]
        q_a, k_a, v_a, rows["mla_latent"], rows["mla_krope"] = _mla_in(
            x, w, tabs, (rows["mla_latent"], rows["mla_krope"]), li, tm, key_major)
        carry = zero_carry
        if past:
            tp = _row_tile(n_past, PAST_ROW_TILE)
            k_a_p, v_a_p = _mla_past(past["mla_latent"], past["mla_krope"], li, w, tp)
            fk_p, carry = _fox_past(past["fox_k"], past["fox_logf"], li, w, tp)
        q_b, k_b, v_b, _, rows["fox_k"], rows["fox_v"], rows["fox_logf"] = _fox_in(
            x, w, carry, (rows["fox_k"], rows["fox_v"], rows["fox_logf"]), li, tm, key_major)
        q_c, k_c, v_c, gates, rows["diff_k"], rows["diff_v"] = _diff_gate_in(
            x, w, (rows["diff_k"], rows["diff_v"]), li, tm, key_major)
        lam_init = 0.8 - 0.6 * math.exp(-0.3 * li)
        if past:
            o_a = _flash(q_a, k_a_p, v_a_p, k_a, v_a, li=li, heads=MLA_HEADS, dv=MLA_V, tk=tk, frame=False,
                         name="mla_flash")
            o_b = _flash(q_b, fk_p, past["fox_v"], k_b, v_b, li=li, heads=FOX_HEADS, dv=FOX_DIM, tk=tk, frame=True,
                         name="fox_flash")
            o_c = _diff_flash(q_c, past["diff_k"], past["diff_v"], k_c, v_c, bias, bias_new,
                              w["diff_lambda"], w["diff_g"], li=li, tk=tk, lam_init=lam_init, bias_index=bias_index)
        else:
            o_a = _flash_t(q_a, k_a, v_a, heads=MLA_HEADS, dv=MLA_V, tq=tq, tk=tk, frame=False, name="mla_flash_t")
            o_b = _flash_t(q_b, k_b, v_b, heads=FOX_HEADS, dv=FOX_DIM, tq=tq, tk=tk, frame=True, name="fox_flash_t")
            o_c = _diff_flash_t(q_c, k_c, v_c, bias, w["diff_lambda"], w["diff_g"], tq=tq, tk=tk,
                                lam_init=lam_init, bias_index=bias_index)
        M = B * n
        flat = lambda a: a.reshape(M, a.shape[-1])
        x1 = _mix_out(flat(x), flat(o_a), flat(o_b), flat(o_c), flat(gates), w, _row_tile(M, MIX_ROW_TILE))
        x2, y = _ffn(x1, w, final_g if li == depth - 1 else None, _row_tile(M, ROW_TILE))
        x = x2.reshape(B, n, D)
    return y.reshape(B, n, D), rows


def kernel(x_prompt, x_sample, cache_mla_latent, cache_mla_krope, cache_fox_k, cache_fox_v, cache_fox_logf,
           cache_diff_k, cache_diff_v, attn_norm_g, w_in, mla_q_norm_g, mla_kv_norm_g, w_mla_uq, w_mla_ukv,
           fox_f_bias, diff_lambda, diff_subln_g, w_branch, w_o, ffn_norm_g, w_ffn_in, w_ffn_out,
           rel_bias_table, final_norm_g):
    params = {
        "attn_norm_g": attn_norm_g, "w_in": w_in, "mla_q_norm_g": mla_q_norm_g,
        "mla_kv_norm_g": mla_kv_norm_g, "w_mla_uq": w_mla_uq, "w_mla_ukv": w_mla_ukv,
        "fox_f_bias": fox_f_bias, "diff_lambda": diff_lambda, "diff_subln_g": diff_subln_g,
        "w_branch": w_branch, "w_o": w_o, "ffn_norm_g": ffn_norm_g, "w_ffn_in": w_ffn_in,
        "w_ffn_out": w_ffn_out,
    }
    past = {
        "mla_latent": cache_mla_latent, "mla_krope": cache_mla_krope, "fox_k": cache_fox_k,
        "fox_v": cache_fox_v, "fox_logf": cache_fox_logf, "diff_k": cache_diff_k, "diff_v": cache_diff_v,
    }
    depth = w_in.shape[0]
    weights = [_prep_layer_weights(li, params) for li in range(depth)]
    final_g = final_norm_g.reshape(1, -1)
    y_p, sp = _run_trunk(x_prompt, {}, weights, rel_bias_table, final_g)
    y_s, ss = _run_trunk(x_sample, past, weights, rel_bias_table, final_g)
    return (y_p, y_s,
            sp["mla_latent"], ss["mla_latent"], sp["mla_krope"], ss["mla_krope"],
            sp["fox_k"], ss["fox_k"], sp["fox_v"], ss["fox_v"],
            sp["fox_logf"], ss["fox_logf"], sp["diff_k"], ss["diff_k"],
            sp["diff_v"], ss["diff_v"])
```

```python
import functools
import math

import jax
import jax.numpy as jnp
from jax import lax
from jax.experimental import pallas as pl
from jax.experimental.pallas import tpu as pltpu

F32 = jnp.float32
BF16 = jnp.bfloat16

CHUNK_SHIFT = 6
NORM_EPS = 1e-6
LOG2E = math.log2(math.e)
NEG_INF = -1e30
MLA_HEADS, MLA_NOPE, MLA_ROPE, MLA_V = 8, 64, 32, 64
MLA_Q_RANK, MLA_KV_RANK = 384, 256
ROPE_BASE = 10000.0
FOX_HEADS, FOX_DIM = 8, 64
DIFF_HEADS, DIFF_QK, DIFF_V = 4, 64, 128
MIX_WIDTH = 512
N_BRANCHES = 3
REL_BUCKETS, REL_MAX_DIST = 32, 128

LANES = 128
VMEM_LIMIT = 56 * 1024 * 1024


def _cparams(sem):
    return pltpu.CompilerParams(dimension_semantics=sem, vmem_limit_bytes=VMEM_LIMIT)


def _rms(x, g):
    return x * lax.rsqrt(jnp.mean(x * x, axis=-1, keepdims=True) + NORM_EPS) * g


def _dot(a, b):
    return jnp.dot(a, b, preferred_element_type=F32)


def _dot_nt(a, b):
    return lax.dot_general(a, b, (((1,), (1,)), ((), ())), preferred_element_type=F32)


def _bf16_part(x):
    return x.astype(BF16).astype(F32)


def _store_v(v_out, v, transposed):
    v_out[...] = (jnp.transpose(v) if transposed else v).astype(BF16)


def _cumsum_rows(x):
    rows = x.shape[0]
    row = lax.broadcasted_iota(jnp.int32, x.shape, 0)
    shift = 1
    while shift < rows:
        x = x + jnp.where(row >= shift, pltpu.roll(x, shift, 0), 0.0)
        shift *= 2
    return x


def _mla_in_kernel(x_ref, ga_ref, win_ref, gq_ref, wuq_ref, gkv_ref, wukv_ref,
                   cq_ref, sq_ref, ck_ref, sk_ref, lat_stack, kr_stack,
                   q_out, k_out, v_out, lat_out, kr_out, *, v_transposed):
    del lat_stack, kr_stack
    hb = _rms(x_ref[...], ga_ref[...]).astype(BF16)
    pm = _dot(hb, win_ref[...])
    cqn = _rms(pm[:, :MLA_Q_RANK], gq_ref[...]).astype(BF16)
    q2 = _dot(cqn, wuq_ref[...])
    hw = MLA_HEADS * LANES
    cosq = jnp.concatenate([cq_ref[...]] * MLA_HEADS, axis=1)
    sinq = jnp.concatenate([sq_ref[...]] * MLA_HEADS, axis=1)
    q_out[...] = (q2[:, :hw] * cosq + q2[:, hw:] * sinq).astype(BF16)
    latent = _rms(pm[:, MLA_Q_RANK:MLA_Q_RANK + MLA_KV_RANK], gkv_ref[...])
    lat_out[...] = latent
    kv2 = _dot(latent.astype(BF16), wukv_ref[...])
    o = MLA_Q_RANK + MLA_KV_RANK
    krb = pm[:, o:o + LANES] * ck_ref[...] + pm[:, o + LANES:o + 2 * LANES] * sk_ref[...]
    kr_out[...] = krb[:, MLA_NOPE:MLA_NOPE + MLA_ROPE]
    k_out[...] = (kv2[:, :hw] + jnp.concatenate([krb] * MLA_HEADS, axis=1)).astype(BF16)
    _store_v(v_out, kv2[:, hw:], v_transposed)


def _decay_columns(c, rows):
    hi = _bf16_part(c)
    r1 = c - hi
    mid = _bf16_part(r1)
    lo = _bf16_part(r1 - mid)
    lane = lax.broadcasted_iota(jnp.int32, (rows, LANES), 1)
    d = FOX_DIM
    ones_q = jnp.where((lane >= d + 3) & (lane < d + 6), 1.0, 0.0)
    ones_k = jnp.where((lane >= d) & (lane < d + 3), 1.0, 0.0)
    eq, ek = [], []
    for h in range(FOX_HEADS):
        bh = jnp.broadcast_to(hi[:, h:h + 1], (rows, LANES))
        bm = jnp.broadcast_to(mid[:, h:h + 1], (rows, LANES))
        bl = jnp.broadcast_to(lo[:, h:h + 1], (rows, LANES))
        eq.append(jnp.where(lane == d, bh, jnp.where(lane == d + 1, bm, jnp.where(lane == d + 2, bl, ones_q))))
        ek.append(jnp.where(lane == d + 3, -bh, jnp.where(lane == d + 4, -bm, jnp.where(lane == d + 5, -bl, ones_k))))
    return jnp.concatenate(eq, axis=1), jnp.concatenate(ek, axis=1)


def _store_heads(out_ref, x, heads):
    width = x.shape[1] // heads
    for h in range(heads):
        out_ref[:, h, :] = x[:, h * width:(h + 1) * width]


def _fox_in_kernel(x_ref, ga_ref, win_ref, fb_ref, sel_ref, c0_ref, kf_stack, vf_stack, lf_stack,
                   q_out, k_out, vb_out, ct_out, kf_out, vf_out, lf_out, carry_sc, *, v_transposed):
    del kf_stack, vf_stack, lf_stack
    rows = x_ref.shape[0]
    t = pl.program_id(1)

    @pl.when(t == 0)
    def _():
        carry_sc[...] = c0_ref[...]

    hb = _rms(x_ref[...], ga_ref[...]).astype(BF16)
    pf = _dot(hb, win_ref[...])
    w = FOX_HEADS * FOX_DIM
    fq = pf[:, :w] * (FOX_DIM ** -0.5 * LOG2E)
    fk = pf[:, w:2 * w]
    fv = pf[:, 2 * w:3 * w]
    z = pf[:, 3 * w:3 * w + LANES] + fb_ref[...]
    lane = lax.broadcasted_iota(jnp.int32, (rows, LANES), 1)
    logf = jnp.minimum(z, 0.0) - jnp.log1p(jnp.exp(-jnp.abs(z)))
    logf = jnp.where(lane < FOX_HEADS, logf, 0.0)
    lf_out[...] = logf[:, :FOX_HEADS]
    c = carry_sc[...] + _cumsum_rows(logf)
    carry_sc[...] = c[rows - 1:rows, :]
    ct_out[...] = c[rows - 1:rows, :]
    if v_transposed:
        fk_t, fv_t = jnp.transpose(fk), jnp.transpose(fv)
        for h in range(FOX_HEADS):
            kf_out[h] = fk_t[h * FOX_DIM:(h + 1) * FOX_DIM, :]
            vf_out[h] = fv_t[h * FOX_DIM:(h + 1) * FOX_DIM, :]
        vb_out[...] = fv_t.astype(BF16)
    else:
        _store_heads(kf_out, fk, FOX_HEADS)
        _store_heads(vf_out, fv, FOX_HEADS)
        vb_out[...] = fv.astype(BF16)
    eq, ek = _decay_columns(c * LOG2E, rows)
    q_out[...] = (_dot(fq.astype(BF16), sel_ref[...]) + eq).astype(BF16)
    k_out[...] = (_dot(fk.astype(BF16), sel_ref[...]) + ek).astype(BF16)


def _diff_gate_in_kernel(x_ref, ga_ref, wd_ref, wg_ref, dk_stack, dv_stack,
                         dq_out, dkb_out, dvb_out, gate_out, dk_out, dv_out, *, v_transposed):
    del dk_stack, dv_stack
    hb = _rms(x_ref[...], ga_ref[...]).astype(BF16)
    pd = _dot(hb, wd_ref[...])
    w = DIFF_HEADS * 2 * DIFF_QK
    dq_out[...] = (pd[:, :w] * (DIFF_QK ** -0.5 * LOG2E)).astype(BF16)
    dk = pd[:, w:2 * w]
    dv = pd[:, 2 * w:]
    _store_heads(dk_out, dk, DIFF_HEADS)
    _store_heads(dv_out, dv, DIFF_HEADS)
    dkb_out[...] = dk.astype(BF16)
    _store_v(dvb_out, dv, v_transposed)
    gate_out[...] = jax.nn.sigmoid(_dot(hb, wg_ref[...]))


def _mla_past_kernel(lat_ref, kr_ref, wukv_ref, selk_ref, k_out, v_out):
    kv2 = _dot(lat_ref[...].astype(BF16), wukv_ref[...])
    krb = _dot(kr_ref[...].astype(BF16), selk_ref[...])
    hw = MLA_HEADS * LANES
    k_out[...] = (kv2[:, :hw] + jnp.concatenate([krb] * MLA_HEADS, axis=1)).astype(BF16)
    v_out[...] = kv2[:, hw:].astype(BF16)


def _fox_past_kernel(k_ref, lf_ref, sel_ref, k_out, ct_out, carry_sc, lf_sc):
    rows = k_ref.shape[0]
    t = pl.program_id(1)

    @pl.when(t == 0)
    def _():
        carry_sc[...] = jnp.zeros(carry_sc.shape, F32)

    lf_sc[...] = jnp.zeros(lf_sc.shape, F32)
    lf_sc[:, :FOX_HEADS] = lf_ref[...]
    c = carry_sc[...] + _cumsum_rows(lf_sc[...])
    carry_sc[...] = c[rows - 1:rows, :]
    ct_out[...] = c[rows - 1:rows, :]
    _, ek = _decay_columns(c * LOG2E, rows)
    k_out[...] = (_dot(k_ref[...].astype(BF16), sel_ref[...]) + ek).astype(BF16)


def _block_visibility(q0, k0, tq, tk, frame):
    if frame:
        return k0 <= q0 + (tq - 1), k0 + (tk - 1) <= q0
    return ((k0 >> CHUNK_SHIFT) <= ((q0 + (tq - 1)) >> CHUNK_SHIFT),
            ((k0 + (tk - 1)) >> CHUNK_SHIFT) <= (q0 >> CHUNK_SHIFT))


def _mask(q0, k0, tq, tk, frame):
    row = q0 + lax.broadcasted_iota(jnp.int32, (tq, tk), 0)
    col = k0 + lax.broadcasted_iota(jnp.int32, (tq, tk), 1)
    if frame:
        return col <= row
    return (col >> CHUNK_SHIFT) <= (row >> CHUNK_SHIFT)


def _online_softmax_step(s, v, idx, m_sc, l_sc, acc_sc):
    dv = v.shape[1]
    m_prev = m_sc[idx]
    m_new = jnp.maximum(m_prev, jnp.max(s, axis=1, keepdims=True))
    alpha = jnp.exp2(m_prev - m_new)
    p = jnp.exp2(s - m_new[:, :1])
    l_sc[idx] = alpha * l_sc[idx] + jnp.sum(p, axis=1, keepdims=True)
    m_sc[idx] = m_new
    acc_sc[idx] = acc_sc[idx] * alpha[:, :dv] + _dot(p.astype(BF16), v)


def _head_cols(ref, h, width):
    if len(ref.shape) == 3:
        return ref[:, h, :]
    return ref[:, h * width:(h + 1) * width]


def _init_softmax_state(m_sc, l_sc, acc_sc):
    m_sc[...] = jnp.full(m_sc.shape, NEG_INF, F32)
    l_sc[...] = jnp.zeros(l_sc.shape, F32)
    acc_sc[...] = jnp.zeros(acc_sc.shape, F32)


def _flash_kernel(q_ref, kp_ref, vp_ref, kn_ref, vn_ref, o_ref, m_sc, l_sc, acc_sc, *, heads, dv, n_past, frame):
    kj = pl.program_id(1)
    n_hist = pl.num_programs(1) - 1
    tq = q_ref.shape[0]

    @pl.when(kj == 0)
    def _():
        _init_softmax_state(m_sc, l_sc, acc_sc)

    def step(k_ref, v_ref, mask):
        scores = lambda h: _dot_nt(q_ref[:, h * LANES:(h + 1) * LANES],
                                   k_ref[:, h * LANES:(h + 1) * LANES].astype(BF16))
        s_next = scores(0)
        for h in range(heads):
            s = s_next
            if h + 1 < heads:
                s_next = scores(h + 1)
            if mask is not None:
                s = jnp.where(mask, s, NEG_INF)
            _online_softmax_step(s, _head_cols(v_ref, h, dv).astype(BF16), h, m_sc, l_sc, acc_sc)

    @pl.when(kj < n_hist)
    def _():
        step(kp_ref, vp_ref, None)

    @pl.when(kj == n_hist)
    def _():
        step(kn_ref, vn_ref, _mask(n_past, n_past, tq, kn_ref.shape[0], frame))
        for h in range(heads):
            o_ref[:, h * dv:(h + 1) * dv] = (acc_sc[h] / l_sc[h][:, :dv]).astype(o_ref.dtype)


def _diff_flash_kernel(lam_ref, g_ref, q_ref, kp_ref, vp_ref, kn_ref, vn_ref, bp_ref, bn_ref, o_ref,
                       m_sc, l_sc, acc_sc, *, n_past, lam_init):
    kj = pl.program_id(1)
    n_hist = pl.num_programs(1) - 1
    tq = q_ref.shape[0]

    @pl.when(kj == 0)
    def _():
        _init_softmax_state(m_sc, l_sc, acc_sc)

    def step(k_ref, v_ref, b_ref, mask):
        lane = lax.broadcasted_iota(jnp.int32, (tq, LANES), 1)

        def scores(i):
            h, c = divmod(i, 2)
            q = q_ref[:, h * LANES:(h + 1) * LANES]
            qc = jnp.where((lane < DIFF_QK) if c == 0 else (lane >= DIFF_QK), q, jnp.zeros_like(q))
            return _dot_nt(qc, _head_cols(k_ref, h, LANES).astype(BF16))

        n_maps = 2 * DIFF_HEADS
        s_next = scores(0)
        for i in range(n_maps):
            s = s_next + b_ref[i // 2]
            if i + 1 < n_maps:
                s_next = scores(i + 1)
            if mask is not None:
                s = jnp.where(mask, s, NEG_INF)
            v = _head_cols(v_ref, i // 2, LANES).astype(BF16)
            _online_softmax_step(s, v, i, m_sc, l_sc, acc_sc)

    @pl.when(kj < n_hist)
    def _():
        step(kp_ref, vp_ref, bp_ref, None)

    @pl.when(kj == n_hist)
    def _():
        step(kn_ref, vn_ref, bn_ref, _mask(n_past, n_past, tq, kn_ref.shape[0], False))
        lp = lam_ref[...]
        lam = (jnp.exp(jnp.sum(lp[0:1, :] * lp[1:2, :], axis=1, keepdims=True))
               - jnp.exp(jnp.sum(lp[2:3, :] * lp[3:4, :], axis=1, keepdims=True)) + lam_init)
        g = g_ref[...]
        for h in range(DIFF_HEADS):
            o = acc_sc[2 * h] / l_sc[2 * h] - lam * (acc_sc[2 * h + 1] / l_sc[2 * h + 1])
            o_ref[:, h * LANES:(h + 1) * LANES] = (_rms(o, g) * (1.0 - lam_init)).astype(o_ref.dtype)


def _mask_t(q0, k0, tq, tk, frame):
    key = k0 + lax.broadcasted_iota(jnp.int32, (tk, tq), 0)
    qry = q0 + lax.broadcasted_iota(jnp.int32, (tk, tq), 1)
    if frame:
        return key <= qry
    return (key >> CHUNK_SHIFT) <= (qry >> CHUNK_SHIFT)


def _softmax_weights_t(st, idx, m_sc):
    m_prev = m_sc[idx]
    m_new = jnp.maximum(m_prev, jnp.max(st, axis=0, keepdims=True))
    m_sc[idx] = m_new
    return jnp.exp2(st - m_new).astype(BF16), jnp.exp2(m_prev - m_new)


def _accumulate_t(p, alpha, vt, idx, acc_sc):
    acc_sc[idx] = acc_sc[idx] * alpha + _dot(vt, p)


def _pipelined_maps_t(n_maps, scores, logits, values, m_sc, acc_sc):
    st_next = scores(0)
    pending = None
    for i in range(n_maps + 1):
        st = st_next
        if i + 1 < n_maps:
            st_next = scores(i + 1)
        current = (_softmax_weights_t(logits(i, st), i, m_sc) + (i,)) if i < n_maps else None
        if pending is not None:
            p, alpha, j = pending
            _accumulate_t(p, alpha, values(j), j, acc_sc)
        pending = current


def _values_with_ones(vt_ref, h, dv, tk):
    return jnp.concatenate([vt_ref[h * dv:(h + 1) * dv, :], jnp.ones((BF16_SUBLANES, tk), BF16)], axis=0)


def _normalised(acc, dv):
    return acc[:dv, :] / acc[dv:dv + 1, :]


def _flash_t_kernel(q_ref, k_ref, vt_ref, o_ref, m_sc, acc_sc, *, heads, dv, tq, tk, cw, frame):
    qi, kj = pl.program_id(1), pl.program_id(2)
    q0 = qi * tq
    k0 = kj * tk
    n_col = tq // cw

    @pl.when(kj == 0)
    def _():
        m_sc[...] = jnp.full(m_sc.shape, NEG_INF, F32)
        acc_sc[...] = jnp.zeros(acc_sc.shape, F32)

    needed, full = _block_visibility(q0, k0, tq, tk, frame)

    def step(masked):
        mask = _mask_t(q0, k0, tq, tk, frame) if masked else None

        def scores(i):
            h, c = divmod(i, n_col)
            return _dot_nt(k_ref[:, h * LANES:(h + 1) * LANES], q_ref[c * cw:(c + 1) * cw, h * LANES:(h + 1) * LANES])

        def logits(i, st):
            c = i % n_col
            return jnp.where(mask[:, c * cw:(c + 1) * cw], st, NEG_INF) if masked else st

        values = lambda i: _values_with_ones(vt_ref, i // n_col, dv, tk)
        _pipelined_maps_t(heads * n_col, scores, logits, values, m_sc, acc_sc)

    @pl.when(needed & full)
    def _():
        step(False)

    @pl.when(needed & jnp.logical_not(full))
    def _():
        step(True)

    @pl.when(kj == pl.num_programs(2) - 1)
    def _():
        per_blk = LANES // dv
        for g in range(heads // per_blk):
            for c in range(n_col):
                rows = [_normalised(acc_sc[(g * per_blk + i) * n_col + c], dv) for i in range(per_blk)]
                blk = rows[0] if per_blk == 1 else jnp.concatenate(rows, axis=0)
                o_ref[c * cw:(c + 1) * cw, g * LANES:(g + 1) * LANES] = jnp.transpose(blk).astype(o_ref.dtype)


def _diff_flash_t_kernel(lam_ref, g_ref, q_ref, k_ref, vt_ref, b_ref, o_ref, m_sc, acc_sc,
                         *, tq, tk, lam_init):
    qi, kj = pl.program_id(1), pl.program_id(2)
    q0 = qi * tq
    k0 = kj * tk

    @pl.when(kj == 0)
    def _():
        m_sc[...] = jnp.full(m_sc.shape, NEG_INF, F32)
        acc_sc[...] = jnp.zeros(acc_sc.shape, F32)

    needed, full = _block_visibility(q0, k0, tq, tk, False)

    def step(masked):
        mask = _mask_t(q0, k0, tq, tk, False) if masked else None
        lane = lax.broadcasted_iota(jnp.int32, (tq, LANES), 1)

        def scores(i):
            h, c = divmod(i, 2)
            q = q_ref[:, h * LANES:(h + 1) * LANES]
            qc = jnp.where((lane < DIFF_QK) if c == 0 else (lane >= DIFF_QK), q, jnp.zeros_like(q))
            return _dot_nt(k_ref[:, h * LANES:(h + 1) * LANES], qc)

        def logits(i, st):
            st = st + b_ref[i // 2]
            return jnp.where(mask, st, NEG_INF) if masked else st

        values = lambda i: _values_with_ones(vt_ref, i // 2, DIFF_V, tk)
        _pipelined_maps_t(2 * DIFF_HEADS, scores, logits, values, m_sc, acc_sc)

    @pl.when(needed & full)
    def _():
        step(False)

    @pl.when(needed & jnp.logical_not(full))
    def _():
        step(True)

    @pl.when(kj == pl.num_programs(2) - 1)
    def _():
        lp = lam_ref[...]
        lam = (jnp.exp(jnp.sum(lp[0:1, :] * lp[1:2, :], axis=1, keepdims=True))
               - jnp.exp(jnp.sum(lp[2:3, :] * lp[3:4, :], axis=1, keepdims=True)) + lam_init)
        g = g_ref[...]
        for h in range(DIFF_HEADS):
            o_t = _normalised(acc_sc[2 * h], DIFF_V) - lam * _normalised(acc_sc[2 * h + 1], DIFF_V)
            o = jnp.transpose(o_t)
            o_ref[:, h * LANES:(h + 1) * LANES] = (_rms(o, g) * (1.0 - lam_init)).astype(o_ref.dtype)


def _mix_out_kernel(x_ref, oa_ref, ob_ref, oc_ref, gate_ref, wb_ref, wo_ref, x_out):
    d = x_ref.shape[1]
    mix = None
    for j, o_ref in enumerate((oa_ref, ob_ref, oc_ref)):
        term = gate_ref[:, j * d:(j + 1) * d] * _dot(o_ref[...], wb_ref[j])
        mix = term if mix is None else mix + term
    x_out[...] = x_ref[...] + _dot(mix.astype(BF16), wo_ref[...])


def _ffn_kernel(x_ref, g_ref, win_ref, wout_ref, gfin_ref, x_out, y_out=None):
    x = x_ref[...]
    gu = _dot(_rms(x, g_ref[...]).astype(BF16), win_ref[...])
    dff = wout_ref.shape[0]
    gate, up = gu[:, :dff], gu[:, dff:]
    x2 = x + _dot((gate * jax.nn.sigmoid(gate) * up).astype(BF16), wout_ref[...])
    x_out[...] = x2
    if y_out is not None:
        y_out[...] = _rms(x2, gfin_ref[...])


BF16_SUBLANES = 16
ROW_TILE = 256
MIX_ROW_TILE = 512
PAST_ROW_TILE = 512
PROMPT_Q_TILE = 1024
PROMPT_K_TILE = 512
PROMPT_Q_COLS = 1024
SAMPLE_KEY_TILE = 2048


def _row_tile(n, target):
    for t in range(min(n, target), 0, -1):
        if n % t == 0 and (t % BF16_SUBLANES == 0 or t == n):
            return t
    raise ValueError((n, target))


def _full_spec(arr):
    nd = arr.ndim
    return pl.BlockSpec(arr.shape, lambda *_: (0,) * nd)


def _rows3(width, tm):
    return pl.BlockSpec((None, tm, width), lambda b, t: (b, t, 0))


def _layer_rows_spec(arr, li, tm):
    tail = arr.shape[3:]
    return pl.BlockSpec((None, None, tm) + tail, lambda b, t: (li, b, t) + (0,) * len(tail))


ALIASED = pl.BlockSpec(memory_space=pl.ANY)


def _work_specs(outs, B, n, tm, v_index, v_t):
    specs = [_rows3(wd, tm) for wd, _ in outs]
    shapes = [jax.ShapeDtypeStruct((B, n, wd), dt) for wd, dt in outs]
    if v_t:
        wd, dt = outs[v_index]
        specs[v_index] = pl.BlockSpec((None, wd, tm), lambda b, t: (b, 0, t))
        shapes[v_index] = jax.ShapeDtypeStruct((B, wd, n), dt)
    return specs, shapes


def _layer_cols_spec(arr, li, tm):
    mid = arr.shape[2:-1]
    return pl.BlockSpec((None, None) + mid + (tm,), lambda b, t: (li, b) + (0,) * len(mid) + (t,))


def _stacked_outputs(stacks, li, tm, n_inputs, n_work_outputs, feature_major=()):
    specs = [(_layer_cols_spec if i in feature_major else _layer_rows_spec)(a, li, tm) for i, a in enumerate(stacks)]
    shapes = [jax.ShapeDtypeStruct(a.shape, a.dtype) for a in stacks]
    aliases = {n_inputs + i: n_work_outputs + i for i in range(len(stacks))}
    return specs, shapes, aliases


def _mla_in(x, w, tabs, stacks, li, tm, v_t):
    B, n, D = x.shape
    hw = MLA_HEADS * LANES
    tab_spec = pl.BlockSpec((tm, LANES), lambda b, t: (t, 0))
    consts = (w["attn_g"], w["mla_in"], w["q_g"], w["uq2"], w["kv_g"], w["ukv2"])
    outs = [(hw, BF16), (hw, BF16), (MLA_HEADS * MLA_V, BF16)]
    specs, shapes = _work_specs(outs, B, n, tm, 2, v_t)
    n_in = 1 + len(consts) + len(tabs)
    s_specs, s_shapes, aliases = _stacked_outputs(stacks, li, tm, n_in, len(outs))
    return pl.pallas_call(
        functools.partial(_mla_in_kernel, v_transposed=v_t),
        grid=(B, n // tm),
        in_specs=[_rows3(D, tm)] + [_full_spec(a) for a in consts] + [tab_spec] * 4 + [ALIASED] * len(stacks),
        out_specs=specs + s_specs,
        out_shape=shapes + s_shapes,
        input_output_aliases=aliases,
        compiler_params=_cparams(("parallel", "parallel")),
        name="mla_in",
    )(x, *consts, *tabs, *stacks)


def _fox_in(x, w, c0, stacks, li, tm, v_t):
    B, n, D = x.shape
    hw = FOX_HEADS * LANES
    wd = FOX_HEADS * FOX_DIM
    consts = (w["attn_g"], w["fox_in"], w["fox_bias"], w["fox_sel"])
    row1 = pl.BlockSpec((None, 1, LANES), lambda b, t: (b, 0, 0))
    outs = [(hw, BF16), (hw, BF16), (wd, BF16)]
    specs, shapes = _work_specs(outs, B, n, tm, 2, v_t)
    n_in = 1 + len(consts) + 1
    s_specs, s_shapes, aliases = _stacked_outputs(stacks, li, tm, n_in, len(outs) + 1,
                                                  feature_major=(0, 1) if v_t else ())
    return pl.pallas_call(
        functools.partial(_fox_in_kernel, v_transposed=v_t),
        grid=(B, n // tm),
        in_specs=[_rows3(D, tm)] + [_full_spec(a) for a in consts] + [row1] + [ALIASED] * len(stacks),
        out_specs=specs + [row1] + s_specs,
        out_shape=shapes + [jax.ShapeDtypeStruct((B, 1, LANES), F32)] + s_shapes,
        input_output_aliases=aliases,
        scratch_shapes=[pltpu.VMEM((1, LANES), F32)],
        compiler_params=_cparams(("arbitrary", "arbitrary")),
        name="fox_in",
    )(x, *consts, c0, *stacks)


def _diff_gate_in(x, w, stacks, li, tm, v_t):
    B, n, D = x.shape
    wd = DIFF_HEADS * DIFF_V
    consts = (w["attn_g"], w["diff_in"], w["gate_in"])
    outs = [(wd, BF16), (wd, BF16), (wd, BF16), (N_BRANCHES * D, F32)]
    specs, shapes = _work_specs(outs, B, n, tm, 2, v_t)
    n_in = 1 + len(consts)
    s_specs, s_shapes, aliases = _stacked_outputs(stacks, li, tm, n_in, len(outs))
    return pl.pallas_call(
        functools.partial(_diff_gate_in_kernel, v_transposed=v_t),
        grid=(B, n // tm),
        in_specs=[_rows3(D, tm)] + [_full_spec(a) for a in consts] + [ALIASED] * len(stacks),
        out_specs=specs + s_specs,
        out_shape=shapes + s_shapes,
        input_output_aliases=aliases,
        compiler_params=_cparams(("parallel", "parallel")),
        name="diff_gate_in",
    )(x, *consts, *stacks)


def _mla_past(lat, kr, li, w, tm):
    _, B, n, _ = lat.shape
    hw = MLA_HEADS * LANES
    consts = (w["ukv2"], w["krope_sel"])
    return pl.pallas_call(
        _mla_past_kernel,
        grid=(B, n // tm),
        in_specs=[_layer_rows_spec(lat, li, tm), _layer_rows_spec(kr, li, tm)] + [_full_spec(a) for a in consts],
        out_specs=[_rows3(hw, tm), _rows3(MLA_HEADS * MLA_V, tm)],
        out_shape=[jax.ShapeDtypeStruct((B, n, hw), BF16), jax.ShapeDtypeStruct((B, n, MLA_HEADS * MLA_V), BF16)],
        compiler_params=_cparams(("parallel", "parallel")),
        name="mla_past",
    )(lat, kr, *consts)


def _fox_past(k, lf, li, w, tm):
    B, n, wd = k.shape
    hw = FOX_HEADS * LANES
    row1 = pl.BlockSpec((None, 1, LANES), lambda b, t: (b, 0, 0))
    return pl.pallas_call(
        _fox_past_kernel,
        grid=(B, n // tm),
        in_specs=[_rows3(wd, tm), _layer_rows_spec(lf, li, tm), _full_spec(w["fox_sel"])],
        out_specs=[_rows3(hw, tm), row1],
        out_shape=[jax.ShapeDtypeStruct((B, n, hw), BF16), jax.ShapeDtypeStruct((B, 1, LANES), F32)],
        scratch_shapes=[pltpu.VMEM((1, LANES), F32), pltpu.VMEM((tm, LANES), F32)],
        compiler_params=_cparams(("arbitrary", "arbitrary")),
        name="fox_past",
    )(k, lf, w["fox_sel"])


def _last_needed_block(qi, tq, tk, q_off, nk, frame):
    last_pos = q_off + qi * tq + (tq - 1)
    if not frame:
        last_pos = ((last_pos >> CHUNK_SHIFT) << CHUNK_SHIFT) + ((1 << CHUNK_SHIFT) - 1)
    return jnp.minimum(last_pos // tk, nk - 1)


def _hist_spec(arr, tk, n_hist, li):
    if arr.ndim == 3:
        return pl.BlockSpec((None, tk, arr.shape[2]), lambda b, kj: (b, jnp.minimum(kj, n_hist - 1), 0))
    tail = arr.shape[3:]
    return pl.BlockSpec((None, None, tk) + tail,
                        lambda b, kj: (li, b, jnp.minimum(kj, n_hist - 1)) + (0,) * len(tail))


def _hist_rows(arr):
    return arr.shape[1] if arr.ndim == 3 else arr.shape[2]


def _per_batch_spec(arr):
    return pl.BlockSpec((None,) + arr.shape[1:], lambda b, kj: (b, 0, 0))


def _flash(q, k_past, v_past, k_new, v_new, *, li, heads, dv, tk, frame, name):
    B, n, _ = q.shape
    n_past = _hist_rows(k_past)
    n_hist = n_past // tk
    kern = functools.partial(_flash_kernel, heads=heads, dv=dv, n_past=n_past, frame=frame)
    return pl.pallas_call(
        kern,
        grid=(B, n_hist + 1),
        in_specs=[_per_batch_spec(q), _hist_spec(k_past, tk, n_hist, li), _hist_spec(v_past, tk, n_hist, li),
                  _per_batch_spec(k_new), _per_batch_spec(v_new)],
        out_specs=pl.BlockSpec((None, n, heads * dv), lambda b, kj: (b, 0, 0)),
        out_shape=jax.ShapeDtypeStruct((B, n, heads * dv), BF16),
        scratch_shapes=[pltpu.VMEM((heads, n, LANES), F32), pltpu.VMEM((heads, n, LANES), F32),
                        pltpu.VMEM((heads, n, dv), F32)],
        compiler_params=_cparams(("parallel", "arbitrary")),
        name=name,
    )(q, k_past, v_past, k_new, v_new)


def _diff_flash(q, k_past, v_past, k_new, v_new, bias_past, bias_new, lam_p, g_sub, *, li, tk, lam_init,
                bias_index):
    B, n, w = q.shape
    n_past = _hist_rows(k_past)
    n_hist = n_past // tk
    kern = functools.partial(_diff_flash_kernel, n_past=n_past, lam_init=lam_init)
    return pl.pallas_call(
        kern,
        grid=(B, n_hist + 1),
        in_specs=[_full_spec(lam_p), _full_spec(g_sub), _per_batch_spec(q),
                  _hist_spec(k_past, tk, n_hist, li), _hist_spec(v_past, tk, n_hist, li),
                  _per_batch_spec(k_new), _per_batch_spec(v_new),
                  pl.BlockSpec((None, DIFF_HEADS, n, tk), lambda b, kj: (bias_index(0, kj), 0, 0, 0)),
                  pl.BlockSpec((None, DIFF_HEADS, n, n), lambda b, kj: (0, 0, 0, 0))],
        out_specs=pl.BlockSpec((None, n, w), lambda b, kj: (b, 0, 0)),
        out_shape=jax.ShapeDtypeStruct((B, n, w), BF16),
        scratch_shapes=[pltpu.VMEM((2 * DIFF_HEADS, n, LANES), F32), pltpu.VMEM((2 * DIFF_HEADS, n, LANES), F32),
                        pltpu.VMEM((2 * DIFF_HEADS, n, DIFF_V), F32)],
        compiler_params=_cparams(("parallel", "arbitrary")),
        name="diff_flash",
    )(lam_p, g_sub, q, k_past, v_past, k_new, v_new, bias_past, bias_new)


def _flash_t(q, k, vt, *, heads, dv, tq, tk, frame, name):
    B, n, qw = q.shape
    nq, nk = n // tq, n // tk

    def last(qi):
        return _last_needed_block(qi, tq, tk, 0, nk, frame)

    cw = min(tq, PROMPT_Q_COLS)
    n_maps = heads * (tq // cw)
    kern = functools.partial(_flash_t_kernel, heads=heads, dv=dv, tq=tq, tk=tk, cw=cw, frame=frame)
    return pl.pallas_call(
        kern,
        grid=(B, nq, nk),
        in_specs=[pl.BlockSpec((None, tq, qw), lambda b, qi, kj: (b, qi, 0)),
                  pl.BlockSpec((None, tk, qw), lambda b, qi, kj: (b, jnp.minimum(kj, last(qi)), 0)),
                  pl.BlockSpec((None, heads * dv, tk), lambda b, qi, kj: (b, 0, jnp.minimum(kj, last(qi))))],
        out_specs=pl.BlockSpec((None, tq, heads * dv), lambda b, qi, kj: (b, qi, 0)),
        out_shape=jax.ShapeDtypeStruct((B, n, heads * dv), BF16),
        scratch_shapes=[pltpu.VMEM((n_maps, 1, cw), F32),
                        pltpu.VMEM((n_maps, dv + BF16_SUBLANES, cw), F32)],
        compiler_params=_cparams(("parallel", "parallel", "arbitrary")),
        name=name,
    )(q, k, vt)


def _diff_flash_t(q, k, vt, bias, lam_p, g_sub, *, tq, tk, lam_init, bias_index):
    B, n, w = q.shape
    nq, nk = n // tq, n // tk

    def last(qi):
        return _last_needed_block(qi, tq, tk, 0, nk, False)

    kern = functools.partial(_diff_flash_t_kernel, tq=tq, tk=tk, lam_init=lam_init)
    return pl.pallas_call(
        kern,
        grid=(B, nq, nk),
        in_specs=[_full_spec(lam_p), _full_spec(g_sub),
                  pl.BlockSpec((None, tq, w), lambda b, qi, kj: (b, qi, 0)),
                  pl.BlockSpec((None, tk, w), lambda b, qi, kj: (b, jnp.minimum(kj, last(qi)), 0)),
                  pl.BlockSpec((None, w, tk), lambda b, qi, kj: (b, 0, jnp.minimum(kj, last(qi)))),
                  pl.BlockSpec((None, DIFF_HEADS, tk, tq), lambda b, qi, kj: (bias_index(qi, kj), 0, 0, 0))],
        out_specs=pl.BlockSpec((None, tq, w), lambda b, qi, kj: (b, qi, 0)),
        out_shape=jax.ShapeDtypeStruct((B, n, w), BF16),
        scratch_shapes=[pltpu.VMEM((2 * DIFF_HEADS, 1, tq), F32),
                        pltpu.VMEM((2 * DIFF_HEADS, DIFF_V + BF16_SUBLANES, tq), F32)],
        compiler_params=_cparams(("parallel", "parallel", "arbitrary")),
        name="diff_flash_t",
    )(lam_p, g_sub, q, k, vt, bias)


def _mix_out(x, oa, ob, oc, gates, w, tm):
    M, D = x.shape
    rows = lambda wd: pl.BlockSpec((tm, wd), lambda t: (t, 0))
    return pl.pallas_call(
        _mix_out_kernel,
        grid=(M // tm,),
        in_specs=[rows(D), rows(MIX_WIDTH), rows(MIX_WIDTH), rows(MIX_WIDTH), rows(N_BRANCHES * D),
                  _full_spec(w["branch"]), _full_spec(w["o"])],
        out_specs=rows(D),
        out_shape=jax.ShapeDtypeStruct((M, D), F32),
        compiler_params=_cparams(("parallel",)),
        name="mix_out",
    )(x, oa, ob, oc, gates, w["branch"], w["o"])


def _ffn(x, w, final_g, tm):
    M, D = x.shape
    rows = pl.BlockSpec((tm, D), lambda t: (t, 0))
    consts = (w["ffn_g"], w["ffn_in"], w["ffn_out"], final_g)
    last = final_g is not None
    if not last:
        consts = consts[:3] + (w["ffn_g"],)
    out_shape = [jax.ShapeDtypeStruct((M, D), F32)] * (2 if last else 1)
    res = pl.pallas_call(
        _ffn_kernel,
        grid=(M // tm,),
        in_specs=[rows] + [_full_spec(a) for a in consts],
        out_specs=[rows] * len(out_shape),
        out_shape=out_shape,
        compiler_params=_cparams(("parallel",)),
        name="ffn_final" if last else "ffn",
    )(x, *consts)
    return res if last else (res[0], None)


def _rot_cols(w):
    half = MLA_ROPE // 2
    return jnp.concatenate([-w[..., half:], w[..., :half]], axis=-1)


def _head_blocks(w, heads, width):
    k = w.shape[0]
    w = w.reshape(k, heads, width)
    return jnp.pad(w, ((0, 0), (0, 0), (0, LANES - width))).reshape(k, heads * LANES)


def _prep_layer_weights(li, p):
    d = p["w_in"].shape[1]
    sizes = (MLA_Q_RANK, MLA_KV_RANK, MLA_ROPE,
             FOX_HEADS * FOX_DIM, FOX_HEADS * FOX_DIM, FOX_HEADS * FOX_DIM, FOX_HEADS,
             DIFF_HEADS * 2 * DIFF_QK, DIFF_HEADS * 2 * DIFF_QK, DIFF_HEADS * DIFF_V, N_BRANCHES * d)
    offs = [0]
    for s in sizes:
        offs.append(offs[-1] + s)
    win = p["w_in"][li]
    col = lambda i: win[:, offs[i]:offs[i + 1]]
    w_cq, w_ckv, w_kr, w_fq, w_fk, w_fv, w_ff, w_dq, w_dk, w_dv, w_gl = (col(i) for i in range(11))

    def rope_block(wr):
        return jnp.pad(wr, ((0, 0), (MLA_NOPE, LANES - MLA_NOPE - MLA_ROPE)))

    mla_in = jnp.concatenate([w_cq, w_ckv, rope_block(w_kr), rope_block(_rot_cols(w_kr))], axis=1)
    uq = p["w_mla_uq"][li].reshape(MLA_Q_RANK, MLA_HEADS, MLA_NOPE + MLA_ROPE)
    uq_nope, uq_rope = uq[..., :MLA_NOPE], uq[..., MLA_NOPE:]
    pad = jnp.zeros((MLA_Q_RANK, MLA_HEADS, LANES - MLA_NOPE - MLA_ROPE), F32)
    uq_cat = jnp.concatenate([uq_nope, uq_rope, pad], axis=-1).reshape(MLA_Q_RANK, MLA_HEADS * LANES)
    uq_rot = jnp.concatenate([jnp.zeros_like(uq_nope), _rot_cols(uq_rope), pad], axis=-1)
    uq2 = jnp.concatenate([uq_cat, uq_rot.reshape(MLA_Q_RANK, MLA_HEADS * LANES)], axis=1)
    ukv = p["w_mla_ukv"][li].reshape(MLA_KV_RANK, MLA_HEADS, MLA_NOPE + MLA_V)
    ukn = _head_blocks(ukv[..., :MLA_NOPE].reshape(MLA_KV_RANK, -1), MLA_HEADS, MLA_NOPE)
    uv = ukv[..., MLA_NOPE:].reshape(MLA_KV_RANK, MLA_HEADS * MLA_V)
    ukv2 = jnp.concatenate([ukn, uv], axis=1)
    fox_in = jnp.concatenate([w_fq, w_fk, w_fv, jnp.pad(w_ff, ((0, 0), (0, LANES - FOX_HEADS)))], axis=1)
    fox_sel = _head_blocks(jnp.eye(FOX_HEADS * FOX_DIM, dtype=F32), FOX_HEADS, FOX_DIM)
    krope_sel = jnp.pad(jnp.eye(MLA_ROPE, dtype=F32), ((0, 0), (MLA_NOPE, LANES - MLA_NOPE - MLA_ROPE)))
    row = lambda a: a.reshape(1, -1)
    return {
        "attn_g": row(p["attn_norm_g"][li]),
        "mla_in": mla_in.astype(BF16),
        "q_g": row(p["mla_q_norm_g"][li]),
        "uq2": uq2.astype(BF16),
        "kv_g": row(p["mla_kv_norm_g"][li]),
        "ukv2": ukv2.astype(BF16),
        "krope_sel": krope_sel.astype(BF16),
        "fox_in": fox_in.astype(BF16),
        "fox_bias": jnp.pad(row(p["fox_f_bias"][li]), ((0, 0), (0, LANES - FOX_HEADS))),
        "fox_sel": fox_sel.astype(BF16),
        "diff_in": jnp.concatenate([w_dq, w_dk, w_dv], axis=1).astype(BF16),
        "gate_in": w_gl.astype(BF16),
        "diff_lambda": p["diff_lambda"][li],
        "diff_g": row(p["diff_subln_g"][li]),
        "branch": p["w_branch"][li].astype(BF16),
        "o": p["w_o"][li].astype(BF16),
        "ffn_g": row(p["ffn_norm_g"][li]),
        "ffn_in": p["w_ffn_in"][li].astype(BF16),
        "ffn_out": p["w_ffn_out"][li].astype(BF16),
    }


def _rope_tables(pos):
    half = MLA_ROPE // 2
    inv = 1.0 / (ROPE_BASE ** (jnp.arange(half, dtype=F32) / half))
    ang = pos.astype(F32)[:, None] * inv[None, :]
    cos, sin = jnp.cos(ang), jnp.sin(ang)
    cos2, sin2 = jnp.concatenate([cos, cos], axis=1), jnp.concatenate([sin, sin], axis=1)
    n = pos.shape[0]
    tail = jnp.zeros((n, LANES - MLA_NOPE - MLA_ROPE), F32)
    scale = (MLA_NOPE + MLA_ROPE) ** -0.5 * LOG2E
    cos_q = jnp.concatenate([jnp.ones((n, MLA_NOPE), F32), cos2, tail], axis=1) * scale
    sin_q = jnp.concatenate([jnp.zeros((n, MLA_NOPE), F32), sin2, tail], axis=1) * scale
    cos_k = jnp.concatenate([jnp.zeros((n, MLA_NOPE), F32), cos2, tail], axis=1)
    sin_k = jnp.concatenate([jnp.zeros((n, MLA_NOPE), F32), sin2, tail], axis=1)
    return cos_q, sin_q, cos_k, sin_k


def _rel_bucket(rel):
    half = REL_BUCKETS // 2
    max_exact = half // 2
    ret = jnp.where(rel > 0, half, 0)
    n = jnp.abs(rel)
    large = max_exact + (jnp.log(jnp.maximum(n, 1).astype(F32) / max_exact)
                         / math.log(REL_MAX_DIST / max_exact) * (half - max_exact)).astype(jnp.int32)
    large = jnp.minimum(large, half - 1)
    return ret + jnp.where(n < max_exact, n, large)


def _bias_tiles(rel_table, tq, tk, deltas, key_major):
    ii = jnp.arange(tq, dtype=jnp.int32)
    jj = jnp.arange(tk, dtype=jnp.int32)
    rel0 = (jj[:, None] - ii[None, :]) if key_major else (jj[None, :] - ii[:, None])
    table = rel_table.astype(F32) * LOG2E
    tiles = []
    for d in deltas:
        bucket = _rel_bucket(rel0 - d)[None]
        tile = jnp.zeros((DIFF_HEADS,) + rel0.shape, F32)
        for b in range(REL_BUCKETS):
            tile = jnp.where(bucket == b, table[b][:, None, None], tile)
        tiles.append(tile)
    return jnp.stack(tiles)


def _bias_plan(n, n_keys, tq, tk, q_off):
    nq, nk = n // tq, n_keys // tk
    step = math.gcd(tq, tk) if nq > 1 else tk
    needed = [q_off + qi * tq - kj * tk for qi in range(nq) for kj in range(nk)
              if ((kj * tk) >> CHUNK_SHIFT) <= ((q_off + qi * tq + tq - 1) >> CHUNK_SHIFT)]
    d_min = min(needed)
    far_from = tk + REL_MAX_DIST - 1
    deltas = list(range(d_min, max(far_from, d_min + 1), step))
    n_near = len(deltas)
    deltas.append(deltas[-1] + step)
    base = (q_off - d_min) // step

    def index(qi, kj):
        return jnp.clip(base + qi * (tq // step) - kj * (tk // step), 0, n_near)

    return deltas, index


def _run_trunk(x, past, weights, rel_table, final_g):
    B, n, D = x.shape
    depth = len(weights)
    n_past = past["mla_latent"].shape[2] if past else 0
    n_keys = n_past + n
    key_major = not past
    if past:
        tq, tk = n, _row_tile(n_past, SAMPLE_KEY_TILE)
        deltas, bias_index = _bias_plan(n, n_past, tq, tk, n_past)
        bias_new = _bias_tiles(rel_table, n, n, [0], False)
    else:
        tq, tk = _row_tile(n, PROMPT_Q_TILE), _row_tile(n, PROMPT_K_TILE)
        deltas, bias_index = _bias_plan(n, n, tq, tk, 0)
    bias = _bias_tiles(rel_table, tq, tk, deltas, key_major)
    tm = _row_tile(n, ROW_TILE)
    pos = n_past + jnp.arange(n, dtype=jnp.int32)
    tabs = _rope_tables(pos)
    zero_carry = jnp.zeros((B, 1, LANES), F32)

    fox_shape = (depth, B, FOX_HEADS, FOX_DIM, n) if key_major else (depth, B, n, FOX_HEADS, FOX_DIM)
    rows = {
        "mla_latent": jnp.zeros((depth, B, n, MLA_KV_RANK), F32),
        "mla_krope": jnp.zeros((depth, B, n, MLA_ROPE), F32),
        "fox_k": jnp.zeros(fox_shape, F32),
        "fox_v": jnp.zeros(fox_shape, F32),
        "fox_logf": jnp.zeros((depth, B, n, FOX_HEADS), F32),
        "diff_k": jnp.zeros((depth, B, n, DIFF_HEADS, 2 * DIFF_QK), F32),
        "diff_v": jnp.zeros((depth, B, n, DIFF_HEADS, DIFF_V), F32),
    }
    y = None
    for li in range(depth):
        w = weights[li]
        q_a, k_a, v_a, rows["mla_latent"], rows["mla_krope"] = _mla_in(
            x, w, tabs, (rows["mla_latent"], rows["mla_krope"]), li, tm, key_major)
        carry = zero_carry
        if past:
            tp = _row_tile(n_past, PAST_ROW_TILE)
            rows_of = lambda name: past[name][li].reshape(B, n_past, -1)
            k_a_p, v_a_p = _mla_past(past["mla_latent"], past["mla_krope"], li, w, tp)
            fk_p, carry = _fox_past(rows_of("fox_k"), past["fox_logf"], li, w, tp)
        q_b, k_b, v_b, _, rows["fox_k"], rows["fox_v"], rows["fox_logf"] = _fox_in(
            x, w, carry, (rows["fox_k"], rows["fox_v"], rows["fox_logf"]), li, tm, key_major)
        q_c, k_c, v_c, gates, rows["diff_k"], rows["diff_v"] = _diff_gate_in(
            x, w, (rows["diff_k"], rows["diff_v"]), li, tm, key_major)
        lam_init = 0.8 - 0.6 * math.exp(-0.3 * li)
        if past:
            o_a = _flash(q_a, k_a_p, v_a_p, k_a, v_a, li=li, heads=MLA_HEADS, dv=MLA_V, tk=tk, frame=False,
                         name="mla_flash")
            o_b = _flash(q_b, fk_p, rows_of("fox_v"), k_b, v_b, li=li, heads=FOX_HEADS, dv=FOX_DIM, tk=tk, frame=True,
                         name="fox_flash")
            o_c = _diff_flash(q_c, rows_of("diff_k"), rows_of("diff_v"), k_c, v_c, bias, bias_new,
                              w["diff_lambda"], w["diff_g"], li=li, tk=tk, lam_init=lam_init, bias_index=bias_index)
        else:
            o_a = _flash_t(q_a, k_a, v_a, heads=MLA_HEADS, dv=MLA_V, tq=tq, tk=tk, frame=False, name="mla_flash_t")
            o_b = _flash_t(q_b, k_b, v_b, heads=FOX_HEADS, dv=FOX_DIM, tq=tq, tk=tk, frame=True, name="fox_flash_t")
            o_c = _diff_flash_t(q_c, k_c, v_c, bias, w["diff_lambda"], w["diff_g"], tq=tq, tk=tk,
                                lam_init=lam_init, bias_index=bias_index)
        M = B * n
        flat = lambda a: a.reshape(M, a.shape[-1])
        x1 = _mix_out(flat(x), flat(o_a), flat(o_b), flat(o_c), flat(gates), w, _row_tile(M, MIX_ROW_TILE))
        x2, y = _ffn(x1, w, final_g if li == depth - 1 else None, _row_tile(M, ROW_TILE))
        x = x2.reshape(B, n, D)
    if key_major:
        for name in ("fox_k", "fox_v"):
            rows[name] = jnp.transpose(rows[name], (0, 1, 4, 2, 3))
    return y.reshape(B, n, D), rows


def kernel(x_prompt, x_sample, cache_mla_latent, cache_mla_krope, cache_fox_k, cache_fox_v, cache_fox_logf,
           cache_diff_k, cache_diff_v, attn_norm_g, w_in, mla_q_norm_g, mla_kv_norm_g, w_mla_uq, w_mla_ukv,
           fox_f_bias, diff_lambda, diff_subln_g, w_branch, w_o, ffn_norm_g, w_ffn_in, w_ffn_out,
           rel_bias_table, final_norm_g):
    params = {
        "attn_norm_g": attn_norm_g, "w_in": w_in, "mla_q_norm_g": mla_q_norm_g,
        "mla_kv_norm_g": mla_kv_norm_g, "w_mla_uq": w_mla_uq, "w_mla_ukv": w_mla_ukv,
        "fox_f_bias": fox_f_bias, "diff_lambda": diff_lambda, "diff_subln_g": diff_subln_g,
        "w_branch": w_branch, "w_o": w_o, "ffn_norm_g": ffn_norm_g, "w_ffn_in": w_ffn_in,
        "w_ffn_out": w_ffn_out,
    }
    past = {
        "mla_latent": cache_mla_latent, "mla_krope": cache_mla_krope, "fox_k": cache_fox_k,
        "fox_v": cache_fox_v, "fox_logf": cache_fox_logf, "diff_k": cache_diff_k, "diff_v": cache_diff_v,
    }
    depth = w_in.shape[0]
    weights = [_prep_layer_weights(li, params) for li in range(depth)]
    final_g = final_norm_g.reshape(1, -1)
    y_p, sp = _run_trunk(x_prompt, {}, weights, rel_bias_table, final_g)
    y_s, ss = _run_trunk(x_sample, past, weights, rel_bias_table, final_g)
    return (y_p, y_s,
            sp["mla_latent"], ss["mla_latent"], sp["mla_krope"], ss["mla_krope"],
            sp["fox_k"], ss["fox_k"], sp["fox_v"], ss["fox_v"],
            sp["fox_logf"], ss["fox_logf"], sp["diff_k"], ss["diff_k"],
            sp["diff_v"], ss["diff_v"])
```

```python
import functools
import math

import jax
import jax.numpy as jnp
from jax import lax
from jax.experimental import pallas as pl
from jax.experimental.pallas import tpu as pltpu

F32 = jnp.float32
BF16 = jnp.bfloat16

CHUNK_SHIFT = 6
NORM_EPS = 1e-6
LOG2E = math.log2(math.e)
NEG_INF = -1e30
MLA_HEADS, MLA_NOPE, MLA_ROPE, MLA_V = 8, 64, 32, 64
MLA_Q_RANK, MLA_KV_RANK = 384, 256
ROPE_BASE = 10000.0
FOX_HEADS, FOX_DIM = 8, 64
DIFF_HEADS, DIFF_QK, DIFF_V = 4, 64, 128
MIX_WIDTH = 512
N_BRANCHES = 3
REL_BUCKETS, REL_MAX_DIST = 32, 128

LANES = 128
VMEM_LIMIT = 56 * 1024 * 1024


def _cparams(sem):
    return pltpu.CompilerParams(dimension_semantics=sem, vmem_limit_bytes=VMEM_LIMIT)


def _rms(x, g):
    return x * lax.rsqrt(jnp.mean(x * x, axis=-1, keepdims=True) + NORM_EPS) * g


def _dot(a, b):
    return jnp.dot(a, b, preferred_element_type=F32)


def _dot_nt(a, b):
    return lax.dot_general(a, b, (((1,), (1,)), ((), ())), preferred_element_type=F32)


def _bf16_part(x):
    return x.astype(BF16).astype(F32)


def _store_v(v_out, v, transposed):
    v_out[...] = (jnp.transpose(v) if transposed else v).astype(BF16)


def _cumsum_rows(x):
    rows = x.shape[0]
    row = lax.broadcasted_iota(jnp.int32, x.shape, 0)
    shift = 1
    while shift < rows:
        x = x + jnp.where(row >= shift, pltpu.roll(x, shift, 0), 0.0)
        shift *= 2
    return x


def _mla_in_kernel(x_ref, ga_ref, win_ref, gq_ref, wuq_ref, gkv_ref, wukv_ref,
                   cq_ref, sq_ref, ck_ref, sk_ref, lat_stack, kr_stack,
                   q_out, k_out, v_out, lat_out, kr_out, *, v_transposed):
    del lat_stack, kr_stack
    hb = _rms(x_ref[...], ga_ref[...]).astype(BF16)
    pm = _dot(hb, win_ref[...])
    cqn = _rms(pm[:, :MLA_Q_RANK], gq_ref[...]).astype(BF16)
    q2 = _dot(cqn, wuq_ref[...])
    hw = MLA_HEADS * LANES
    cosq = jnp.concatenate([cq_ref[...]] * MLA_HEADS, axis=1)
    sinq = jnp.concatenate([sq_ref[...]] * MLA_HEADS, axis=1)
    q_out[...] = (q2[:, :hw] * cosq + q2[:, hw:] * sinq).astype(BF16)
    latent = _rms(pm[:, MLA_Q_RANK:MLA_Q_RANK + MLA_KV_RANK], gkv_ref[...])
    lat_out[...] = latent
    kv2 = _dot(latent.astype(BF16), wukv_ref[...])
    o = MLA_Q_RANK + MLA_KV_RANK
    krb = pm[:, o:o + LANES] * ck_ref[...] + pm[:, o + LANES:o + 2 * LANES] * sk_ref[...]
    kr_out[...] = krb[:, MLA_NOPE:MLA_NOPE + MLA_ROPE]
    k_out[...] = (kv2[:, :hw] + jnp.concatenate([krb] * MLA_HEADS, axis=1)).astype(BF16)
    _store_v(v_out, kv2[:, hw:], v_transposed)


def _decay_columns(c, rows):
    hi = _bf16_part(c)
    r1 = c - hi
    mid = _bf16_part(r1)
    lo = _bf16_part(r1 - mid)
    lane = lax.broadcasted_iota(jnp.int32, (rows, LANES), 1)
    d = FOX_DIM
    ones_q = jnp.where((lane >= d + 3) & (lane < d + 6), 1.0, 0.0)
    ones_k = jnp.where((lane >= d) & (lane < d + 3), 1.0, 0.0)
    eq, ek = [], []
    for h in range(FOX_HEADS):
        bh = jnp.broadcast_to(hi[:, h:h + 1], (rows, LANES))
        bm = jnp.broadcast_to(mid[:, h:h + 1], (rows, LANES))
        bl = jnp.broadcast_to(lo[:, h:h + 1], (rows, LANES))
        eq.append(jnp.where(lane == d, bh, jnp.where(lane == d + 1, bm, jnp.where(lane == d + 2, bl, ones_q))))
        ek.append(jnp.where(lane == d + 3, -bh, jnp.where(lane == d + 4, -bm, jnp.where(lane == d + 5, -bl, ones_k))))
    return jnp.concatenate(eq, axis=1), jnp.concatenate(ek, axis=1)


def _store_heads(out_ref, x, heads):
    width = x.shape[1] // heads
    for h in range(heads):
        out_ref[:, h, :] = x[:, h * width:(h + 1) * width]


def _fox_in_kernel(x_ref, ga_ref, win_ref, fb_ref, sel_ref, c0_ref, kf_stack, vf_stack, lf_stack,
                   q_out, k_out, vb_out, ct_out, kf_out, vf_out, lf_out, carry_sc, *, v_transposed):
    del kf_stack, vf_stack, lf_stack
    rows = x_ref.shape[0]
    t = pl.program_id(1)

    @pl.when(t == 0)
    def _():
        carry_sc[...] = c0_ref[...]

    hb = _rms(x_ref[...], ga_ref[...]).astype(BF16)
    pf = _dot(hb, win_ref[...])
    w = FOX_HEADS * FOX_DIM
    fq = pf[:, :w] * (FOX_DIM ** -0.5 * LOG2E)
    fk = pf[:, w:2 * w]
    fv = pf[:, 2 * w:3 * w]
    z = pf[:, 3 * w:3 * w + LANES] + fb_ref[...]
    lane = lax.broadcasted_iota(jnp.int32, (rows, LANES), 1)
    logf = jnp.minimum(z, 0.0) - jnp.log1p(jnp.exp(-jnp.abs(z)))
    logf = jnp.where(lane < FOX_HEADS, logf, 0.0)
    lf_out[...] = logf[:, :FOX_HEADS]
    c = carry_sc[...] + _cumsum_rows(logf)
    carry_sc[...] = c[rows - 1:rows, :]
    ct_out[...] = c[rows - 1:rows, :]
    if v_transposed:
        fk_t, fv_t = jnp.transpose(fk), jnp.transpose(fv)
        for h in range(FOX_HEADS):
            kf_out[h] = fk_t[h * FOX_DIM:(h + 1) * FOX_DIM, :]
            vf_out[h] = fv_t[h * FOX_DIM:(h + 1) * FOX_DIM, :]
        vb_out[...] = fv_t.astype(BF16)
    else:
        _store_heads(kf_out, fk, FOX_HEADS)
        _store_heads(vf_out, fv, FOX_HEADS)
        vb_out[...] = fv.astype(BF16)
    eq, ek = _decay_columns(c * LOG2E, rows)
    q_out[...] = (_dot(fq.astype(BF16), sel_ref[...]) + eq).astype(BF16)
    k_out[...] = (_dot(fk.astype(BF16), sel_ref[...]) + ek).astype(BF16)


def _diff_gate_in_kernel(x_ref, ga_ref, wd_ref, wg_ref, dk_stack, dv_stack,
                         dq_out, dkb_out, dvb_out, gate_out, dk_out, dv_out, *, v_transposed):
    del dk_stack, dv_stack
    hb = _rms(x_ref[...], ga_ref[...]).astype(BF16)
    pd = _dot(hb, wd_ref[...])
    w = DIFF_HEADS * 2 * DIFF_QK
    dq_out[...] = (pd[:, :w] * (DIFF_QK ** -0.5 * LOG2E)).astype(BF16)
    dk = pd[:, w:2 * w]
    dv = pd[:, 2 * w:]
    _store_heads(dk_out, dk, DIFF_HEADS)
    _store_heads(dv_out, dv, DIFF_HEADS)
    dkb_out[...] = dk.astype(BF16)
    _store_v(dvb_out, dv, v_transposed)
    gate_out[...] = jax.nn.sigmoid(_dot(hb, wg_ref[...]))


def _mla_past_kernel(lat_ref, kr_ref, wukv_ref, selk_ref, k_out, v_out):
    kv2 = _dot(lat_ref[...].astype(BF16), wukv_ref[...])
    krb = _dot(kr_ref[...].astype(BF16), selk_ref[...])
    hw = MLA_HEADS * LANES
    k_out[...] = (kv2[:, :hw] + jnp.concatenate([krb] * MLA_HEADS, axis=1)).astype(BF16)
    v_out[...] = kv2[:, hw:].astype(BF16)


def _fox_past_kernel(k_ref, lf_ref, sel_ref, k_out, ct_out, carry_sc, lf_sc):
    rows = k_ref.shape[0]
    t = pl.program_id(1)

    @pl.when(t == 0)
    def _():
        carry_sc[...] = jnp.zeros(carry_sc.shape, F32)

    lf_sc[...] = jnp.zeros(lf_sc.shape, F32)
    lf_sc[:, :FOX_HEADS] = lf_ref[...]
    c = carry_sc[...] + _cumsum_rows(lf_sc[...])
    carry_sc[...] = c[rows - 1:rows, :]
    ct_out[...] = c[rows - 1:rows, :]
    _, ek = _decay_columns(c * LOG2E, rows)
    k_out[...] = (_dot(k_ref[...].astype(BF16), sel_ref[...]) + ek).astype(BF16)


def _block_visibility(q0, k0, tq, tk, frame):
    if frame:
        return k0 <= q0 + (tq - 1), k0 + (tk - 1) <= q0
    return ((k0 >> CHUNK_SHIFT) <= ((q0 + (tq - 1)) >> CHUNK_SHIFT),
            ((k0 + (tk - 1)) >> CHUNK_SHIFT) <= (q0 >> CHUNK_SHIFT))


def _mask(q0, k0, tq, tk, frame):
    row = q0 + lax.broadcasted_iota(jnp.int32, (tq, tk), 0)
    col = k0 + lax.broadcasted_iota(jnp.int32, (tq, tk), 1)
    if frame:
        return col <= row
    return (col >> CHUNK_SHIFT) <= (row >> CHUNK_SHIFT)


def _online_softmax_step(s, v, idx, m_sc, l_sc, acc_sc):
    dv = v.shape[1]
    m_prev = m_sc[idx]
    m_new = jnp.maximum(m_prev, jnp.max(s, axis=1, keepdims=True))
    alpha = jnp.exp2(m_prev - m_new)
    p = jnp.exp2(s - m_new[:, :1])
    l_sc[idx] = alpha * l_sc[idx] + jnp.sum(p, axis=1, keepdims=True)
    m_sc[idx] = m_new
    acc_sc[idx] = acc_sc[idx] * alpha[:, :dv] + _dot(p.astype(BF16), v)


def _head_cols(ref, h, width):
    if len(ref.shape) == 3:
        return ref[:, h, :]
    return ref[:, h * width:(h + 1) * width]


def _init_softmax_state(m_sc, l_sc, acc_sc):
    m_sc[...] = jnp.full(m_sc.shape, NEG_INF, F32)
    l_sc[...] = jnp.zeros(l_sc.shape, F32)
    acc_sc[...] = jnp.zeros(acc_sc.shape, F32)


def _flash_kernel(q_ref, kp_ref, vp_ref, kn_ref, vn_ref, o_ref, m_sc, l_sc, acc_sc, *, heads, dv, n_past, frame):
    kj = pl.program_id(1)
    n_hist = pl.num_programs(1) - 1
    tq = q_ref.shape[0]

    @pl.when(kj == 0)
    def _():
        _init_softmax_state(m_sc, l_sc, acc_sc)

    def step(k_ref, v_ref, mask):
        scores = lambda h: _dot_nt(q_ref[:, h * LANES:(h + 1) * LANES],
                                   k_ref[:, h * LANES:(h + 1) * LANES].astype(BF16))
        s_next = scores(0)
        for h in range(heads):
            s = s_next
            if h + 1 < heads:
                s_next = scores(h + 1)
            if mask is not None:
                s = jnp.where(mask, s, NEG_INF)
            _online_softmax_step(s, _head_cols(v_ref, h, dv).astype(BF16), h, m_sc, l_sc, acc_sc)

    @pl.when(kj < n_hist)
    def _():
        step(kp_ref, vp_ref, None)

    @pl.when(kj == n_hist)
    def _():
        step(kn_ref, vn_ref, _mask(n_past, n_past, tq, kn_ref.shape[0], frame))
        for h in range(heads):
            o_ref[:, h * dv:(h + 1) * dv] = (acc_sc[h] / l_sc[h][:, :dv]).astype(o_ref.dtype)


def _diff_flash_kernel(lam_ref, g_ref, q_ref, kp_ref, vp_ref, kn_ref, vn_ref, bp_ref, bn_ref, o_ref,
                       m_sc, l_sc, acc_sc, *, n_past, lam_init):
    kj = pl.program_id(1)
    n_hist = pl.num_programs(1) - 1
    tq = q_ref.shape[0]

    @pl.when(kj == 0)
    def _():
        _init_softmax_state(m_sc, l_sc, acc_sc)

    def step(k_ref, v_ref, b_ref, mask):
        lane = lax.broadcasted_iota(jnp.int32, (tq, LANES), 1)

        def scores(i):
            h, c = divmod(i, 2)
            q = q_ref[:, h * LANES:(h + 1) * LANES]
            qc = jnp.where((lane < DIFF_QK) if c == 0 else (lane >= DIFF_QK), q, jnp.zeros_like(q))
            return _dot_nt(qc, _head_cols(k_ref, h, LANES).astype(BF16))

        n_maps = 2 * DIFF_HEADS
        s_next = scores(0)
        for i in range(n_maps):
            s = s_next + b_ref[i // 2]
            if i + 1 < n_maps:
                s_next = scores(i + 1)
            if mask is not None:
                s = jnp.where(mask, s, NEG_INF)
            v = _head_cols(v_ref, i // 2, LANES).astype(BF16)
            _online_softmax_step(s, v, i, m_sc, l_sc, acc_sc)

    @pl.when(kj < n_hist)
    def _():
        step(kp_ref, vp_ref, bp_ref, None)

    @pl.when(kj == n_hist)
    def _():
        step(kn_ref, vn_ref, bn_ref, _mask(n_past, n_past, tq, kn_ref.shape[0], False))
        lp = lam_ref[...]
        lam = (jnp.exp(jnp.sum(lp[0:1, :] * lp[1:2, :], axis=1, keepdims=True))
               - jnp.exp(jnp.sum(lp[2:3, :] * lp[3:4, :], axis=1, keepdims=True)) + lam_init)
        g = g_ref[...]
        for h in range(DIFF_HEADS):
            o = acc_sc[2 * h] / l_sc[2 * h] - lam * (acc_sc[2 * h + 1] / l_sc[2 * h + 1])
            o_ref[:, h * LANES:(h + 1) * LANES] = (_rms(o, g) * (1.0 - lam_init)).astype(o_ref.dtype)


def _mask_t(q0, k0, tq, tk, frame):
    key = k0 + lax.broadcasted_iota(jnp.int32, (tk, tq), 0)
    qry = q0 + lax.broadcasted_iota(jnp.int32, (tk, tq), 1)
    if frame:
        return key <= qry
    return (key >> CHUNK_SHIFT) <= (qry >> CHUNK_SHIFT)


def _softmax_weights_t(st, idx, m_sc):
    m_prev = m_sc[idx]
    m_new = jnp.maximum(m_prev, jnp.max(st, axis=0, keepdims=True))
    m_sc[idx] = m_new
    return jnp.exp2(st - m_new).astype(BF16), jnp.exp2(m_prev - m_new)


def _accumulate_t(p, alpha, vt, idx, acc_sc):
    acc_sc[idx] = acc_sc[idx] * alpha + _dot(vt, p)


def _pipelined_maps_t(n_maps, scores, logits, values, m_sc, acc_sc):
    st_next = scores(0)
    pending = None
    for i in range(n_maps + 1):
        st = st_next
        if i + 1 < n_maps:
            st_next = scores(i + 1)
        current = (_softmax_weights_t(logits(i, st), i, m_sc) + (i,)) if i < n_maps else None
        if pending is not None:
            p, alpha, j = pending
            _accumulate_t(p, alpha, values(j), j, acc_sc)
        pending = current


def _values_with_ones(vt_ref, h, dv, tk):
    return jnp.concatenate([vt_ref[h * dv:(h + 1) * dv, :], jnp.ones((BF16_SUBLANES, tk), BF16)], axis=0)


def _normalised(acc, dv):
    return acc[:dv, :] / acc[dv:dv + 1, :]


def _flash_t_kernel(q_ref, k_ref, vt_ref, o_ref, m_sc, acc_sc, *, heads, dv, tq, tk, cw, frame):
    qi, kj = pl.program_id(1), pl.program_id(2)
    q0 = qi * tq
    k0 = kj * tk
    n_col = tq // cw

    @pl.when(kj == 0)
    def _():
        m_sc[...] = jnp.full(m_sc.shape, NEG_INF, F32)
        acc_sc[...] = jnp.zeros(acc_sc.shape, F32)

    needed, full = _block_visibility(q0, k0, tq, tk, frame)

    def step(masked):
        mask = _mask_t(q0, k0, tq, tk, frame) if masked else None

        def scores(i):
            h, c = divmod(i, n_col)
            return _dot_nt(k_ref[:, h * LANES:(h + 1) * LANES], q_ref[c * cw:(c + 1) * cw, h * LANES:(h + 1) * LANES])

        def logits(i, st):
            c = i % n_col
            return jnp.where(mask[:, c * cw:(c + 1) * cw], st, NEG_INF) if masked else st

        values = lambda i: _values_with_ones(vt_ref, i // n_col, dv, tk)
        _pipelined_maps_t(heads * n_col, scores, logits, values, m_sc, acc_sc)

    @pl.when(needed & full)
    def _():
        step(False)

    @pl.when(needed & jnp.logical_not(full))
    def _():
        step(True)

    @pl.when(kj == pl.num_programs(2) - 1)
    def _():
        per_blk = LANES // dv
        for g in range(heads // per_blk):
            for c in range(n_col):
                rows = [_normalised(acc_sc[(g * per_blk + i) * n_col + c], dv) for i in range(per_blk)]
                blk = rows[0] if per_blk == 1 else jnp.concatenate(rows, axis=0)
                o_ref[c * cw:(c + 1) * cw, g * LANES:(g + 1) * LANES] = jnp.transpose(blk).astype(o_ref.dtype)


def _diff_flash_t_kernel(lam_ref, g_ref, q_ref, k_ref, vt_ref, b_ref, o_ref, m_sc, acc_sc,
                         *, tq, tk, lam_init):
    qi, kj = pl.program_id(1), pl.program_id(2)
    q0 = qi * tq
    k0 = kj * tk

    @pl.when(kj == 0)
    def _():
        m_sc[...] = jnp.full(m_sc.shape, NEG_INF, F32)
        acc_sc[...] = jnp.zeros(acc_sc.shape, F32)

    needed, full = _block_visibility(q0, k0, tq, tk, False)

    def step(masked):
        mask = _mask_t(q0, k0, tq, tk, False) if masked else None
        lane = lax.broadcasted_iota(jnp.int32, (tq, LANES), 1)

        def scores(i):
            h, c = divmod(i, 2)
            q = q_ref[:, h * LANES:(h + 1) * LANES]
            qc = jnp.where((lane < DIFF_QK) if c == 0 else (lane >= DIFF_QK), q, jnp.zeros_like(q))
            return _dot_nt(k_ref[:, h * LANES:(h + 1) * LANES], qc)

        def logits(i, st):
            st = st + b_ref[i // 2]
            return jnp.where(mask, st, NEG_INF) if masked else st

        values = lambda i: _values_with_ones(vt_ref, i // 2, DIFF_V, tk)
        _pipelined_maps_t(2 * DIFF_HEADS, scores, logits, values, m_sc, acc_sc)

    @pl.when(needed & full)
    def _():
        step(False)

    @pl.when(needed & jnp.logical_not(full))
    def _():
        step(True)

    @pl.when(kj == pl.num_programs(2) - 1)
    def _():
        lp = lam_ref[...]
        lam = (jnp.exp(jnp.sum(lp[0:1, :] * lp[1:2, :], axis=1, keepdims=True))
               - jnp.exp(jnp.sum(lp[2:3, :] * lp[3:4, :], axis=1, keepdims=True)) + lam_init)
        g = g_ref[...]
        for h in range(DIFF_HEADS):
            o_t = _normalised(acc_sc[2 * h], DIFF_V) - lam * _normalised(acc_sc[2 * h + 1], DIFF_V)
            o = jnp.transpose(o_t)
            o_ref[:, h * LANES:(h + 1) * LANES] = (_rms(o, g) * (1.0 - lam_init)).astype(o_ref.dtype)


def _mix_out_kernel(x_ref, oa_ref, ob_ref, oc_ref, gate_ref, wb_ref, wo_ref, x_out):
    d = x_ref.shape[1]
    mix = None
    for j, o_ref in enumerate((oa_ref, ob_ref, oc_ref)):
        term = gate_ref[:, j * d:(j + 1) * d] * _dot(o_ref[...], wb_ref[j])
        mix = term if mix is None else mix + term
    x_out[...] = x_ref[...] + _dot(mix.astype(BF16), wo_ref[...])


def _ffn_kernel(x_ref, g_ref, win_ref, wout_ref, gfin_ref, x_out, y_out=None):
    x = x_ref[...]
    gu = _dot(_rms(x, g_ref[...]).astype(BF16), win_ref[...])
    dff = wout_ref.shape[0]
    gate, up = gu[:, :dff], gu[:, dff:]
    x2 = x + _dot((gate * jax.nn.sigmoid(gate) * up).astype(BF16), wout_ref[...])
    x_out[...] = x2
    if y_out is not None:
        y_out[...] = _rms(x2, gfin_ref[...])


BF16_SUBLANES = 16
ROW_TILE = 256
MIX_ROW_TILE = 512
PAST_ROW_TILE = 512
PROMPT_Q_TILE = 1024
PROMPT_K_TILE = 512
PROMPT_Q_COLS = 1024
SAMPLE_KEY_TILE = 2048


def _row_tile(n, target):
    for t in range(min(n, target), 0, -1):
        if n % t == 0 and (t % BF16_SUBLANES == 0 or t == n):
            return t
    raise ValueError((n, target))


def _full_spec(arr):
    nd = arr.ndim
    return pl.BlockSpec(arr.shape, lambda *_: (0,) * nd)


def _rows3(width, tm):
    return pl.BlockSpec((None, tm, width), lambda b, t: (b, t, 0))


def _layer_rows_spec(arr, li, tm):
    tail = arr.shape[3:]
    return pl.BlockSpec((None, None, tm) + tail, lambda b, t: (li, b, t) + (0,) * len(tail))


ALIASED = pl.BlockSpec(memory_space=pl.ANY)


def _work_specs(outs, B, n, tm, v_index, v_t):
    specs = [_rows3(wd, tm) for wd, _ in outs]
    shapes = [jax.ShapeDtypeStruct((B, n, wd), dt) for wd, dt in outs]
    if v_t:
        wd, dt = outs[v_index]
        specs[v_index] = pl.BlockSpec((None, wd, tm), lambda b, t: (b, 0, t))
        shapes[v_index] = jax.ShapeDtypeStruct((B, wd, n), dt)
    return specs, shapes


def _layer_cols_spec(arr, li, tm):
    mid = arr.shape[2:-1]
    return pl.BlockSpec((None, None) + mid + (tm,), lambda b, t: (li, b) + (0,) * len(mid) + (t,))


def _stacked_outputs(stacks, li, tm, n_inputs, n_work_outputs, feature_major=()):
    specs = [(_layer_cols_spec if i in feature_major else _layer_rows_spec)(a, li, tm) for i, a in enumerate(stacks)]
    shapes = [jax.ShapeDtypeStruct(a.shape, a.dtype) for a in stacks]
    aliases = {n_inputs + i: n_work_outputs + i for i in range(len(stacks))}
    return specs, shapes, aliases


def _mla_in(x, w, tabs, stacks, li, tm, v_t):
    B, n, D = x.shape
    hw = MLA_HEADS * LANES
    tab_spec = pl.BlockSpec((tm, LANES), lambda b, t: (t, 0))
    consts = (w["attn_g"], w["mla_in"], w["q_g"], w["uq2"], w["kv_g"], w["ukv2"])
    outs = [(hw, BF16), (hw, BF16), (MLA_HEADS * MLA_V, BF16)]
    specs, shapes = _work_specs(outs, B, n, tm, 2, v_t)
    n_in = 1 + len(consts) + len(tabs)
    s_specs, s_shapes, aliases = _stacked_outputs(stacks, li, tm, n_in, len(outs))
    return pl.pallas_call(
        functools.partial(_mla_in_kernel, v_transposed=v_t),
        grid=(B, n // tm),
        in_specs=[_rows3(D, tm)] + [_full_spec(a) for a in consts] + [tab_spec] * 4 + [ALIASED] * len(stacks),
        out_specs=specs + s_specs,
        out_shape=shapes + s_shapes,
        input_output_aliases=aliases,
        compiler_params=_cparams(("parallel", "parallel")),
        name="mla_in",
    )(x, *consts, *tabs, *stacks)


def _fox_in(x, w, c0, stacks, li, tm, v_t):
    B, n, D = x.shape
    hw = FOX_HEADS * LANES
    wd = FOX_HEADS * FOX_DIM
    consts = (w["attn_g"], w["fox_in"], w["fox_bias"], w["fox_sel"])
    row1 = pl.BlockSpec((None, 1, LANES), lambda b, t: (b, 0, 0))
    outs = [(hw, BF16), (hw, BF16), (wd, BF16)]
    specs, shapes = _work_specs(outs, B, n, tm, 2, v_t)
    n_in = 1 + len(consts) + 1
    s_specs, s_shapes, aliases = _stacked_outputs(stacks, li, tm, n_in, len(outs) + 1,
                                                  feature_major=(0, 1) if v_t else ())
    return pl.pallas_call(
        functools.partial(_fox_in_kernel, v_transposed=v_t),
        grid=(B, n // tm),
        in_specs=[_rows3(D, tm)] + [_full_spec(a) for a in consts] + [row1] + [ALIASED] * len(stacks),
        out_specs=specs + [row1] + s_specs,
        out_shape=shapes + [jax.ShapeDtypeStruct((B, 1, LANES), F32)] + s_shapes,
        input_output_aliases=aliases,
        scratch_shapes=[pltpu.VMEM((1, LANES), F32)],
        compiler_params=_cparams(("arbitrary", "arbitrary")),
        name="fox_in",
    )(x, *consts, c0, *stacks)


def _diff_gate_in(x, w, stacks, li, tm, v_t):
    B, n, D = x.shape
    wd = DIFF_HEADS * DIFF_V
    consts = (w["attn_g"], w["diff_in"], w["gate_in"])
    outs = [(wd, BF16), (wd, BF16), (wd, BF16), (N_BRANCHES * D, F32)]
    specs, shapes = _work_specs(outs, B, n, tm, 2, v_t)
    n_in = 1 + len(consts)
    s_specs, s_shapes, aliases = _stacked_outputs(stacks, li, tm, n_in, len(outs))
    return pl.pallas_call(
        functools.partial(_diff_gate_in_kernel, v_transposed=v_t),
        grid=(B, n // tm),
        in_specs=[_rows3(D, tm)] + [_full_spec(a) for a in consts] + [ALIASED] * len(stacks),
        out_specs=specs + s_specs,
        out_shape=shapes + s_shapes,
        input_output_aliases=aliases,
        compiler_params=_cparams(("parallel", "parallel")),
        name="diff_gate_in",
    )(x, *consts, *stacks)


def _mla_past(lat, kr, li, w, tm):
    _, B, n, _ = lat.shape
    hw = MLA_HEADS * LANES
    consts = (w["ukv2"], w["krope_sel"])
    return pl.pallas_call(
        _mla_past_kernel,
        grid=(B, n // tm),
        in_specs=[_layer_rows_spec(lat, li, tm), _layer_rows_spec(kr, li, tm)] + [_full_spec(a) for a in consts],
        out_specs=[_rows3(hw, tm), _rows3(MLA_HEADS * MLA_V, tm)],
        out_shape=[jax.ShapeDtypeStruct((B, n, hw), BF16), jax.ShapeDtypeStruct((B, n, MLA_HEADS * MLA_V), BF16)],
        compiler_params=_cparams(("parallel", "parallel")),
        name="mla_past",
    )(lat, kr, *consts)


def _fox_past(k, lf, li, w, tm):
    B, n, wd = k.shape
    hw = FOX_HEADS * LANES
    row1 = pl.BlockSpec((None, 1, LANES), lambda b, t: (b, 0, 0))
    return pl.pallas_call(
        _fox_past_kernel,
        grid=(B, n // tm),
        in_specs=[_rows3(wd, tm), _layer_rows_spec(lf, li, tm), _full_spec(w["fox_sel"])],
        out_specs=[_rows3(hw, tm), row1],
        out_shape=[jax.ShapeDtypeStruct((B, n, hw), BF16), jax.ShapeDtypeStruct((B, 1, LANES), F32)],
        scratch_shapes=[pltpu.VMEM((1, LANES), F32), pltpu.VMEM((tm, LANES), F32)],
        compiler_params=_cparams(("arbitrary", "arbitrary")),
        name="fox_past",
    )(k, lf, w["fox_sel"])


def _last_needed_block(qi, tq, tk, q_off, nk, frame):
    last_pos = q_off + qi * tq + (tq - 1)
    if not frame:
        last_pos = ((last_pos >> CHUNK_SHIFT) << CHUNK_SHIFT) + ((1 << CHUNK_SHIFT) - 1)
    return jnp.minimum(last_pos // tk, nk - 1)


def _hist_spec(arr, tk, n_hist, li):
    if arr.ndim == 3:
        return pl.BlockSpec((None, tk, arr.shape[2]), lambda b, kj: (b, jnp.minimum(kj, n_hist - 1), 0))
    tail = arr.shape[3:]
    return pl.BlockSpec((None, None, tk) + tail,
                        lambda b, kj: (li, b, jnp.minimum(kj, n_hist - 1)) + (0,) * len(tail))


def _hist_rows(arr):
    return arr.shape[1] if arr.ndim == 3 else arr.shape[2]


def _per_batch_spec(arr):
    return pl.BlockSpec((None,) + arr.shape[1:], lambda b, kj: (b, 0, 0))


def _flash(q, k_past, v_past, k_new, v_new, *, li, heads, dv, tk, frame, name):
    B, n, _ = q.shape
    n_past = _hist_rows(k_past)
    n_hist = n_past // tk
    kern = functools.partial(_flash_kernel, heads=heads, dv=dv, n_past=n_past, frame=frame)
    return pl.pallas_call(
        kern,
        grid=(B, n_hist + 1),
        in_specs=[_per_batch_spec(q), _hist_spec(k_past, tk, n_hist, li), _hist_spec(v_past, tk, n_hist, li),
                  _per_batch_spec(k_new), _per_batch_spec(v_new)],
        out_specs=pl.BlockSpec((None, n, heads * dv), lambda b, kj: (b, 0, 0)),
        out_shape=jax.ShapeDtypeStruct((B, n, heads * dv), BF16),
        scratch_shapes=[pltpu.VMEM((heads, n, LANES), F32), pltpu.VMEM((heads, n, LANES), F32),
                        pltpu.VMEM((heads, n, dv), F32)],
        compiler_params=_cparams(("parallel", "arbitrary")),
        name=name,
    )(q, k_past, v_past, k_new, v_new)


def _diff_flash(q, k_past, v_past, k_new, v_new, bias_past, bias_new, lam_p, g_sub, *, li, tk, lam_init,
                bias_index):
    B, n, w = q.shape
    n_past = _hist_rows(k_past)
    n_hist = n_past // tk
    kern = functools.partial(_diff_flash_kernel, n_past=n_past, lam_init=lam_init)
    return pl.pallas_call(
        kern,
        grid=(B, n_hist + 1),
        in_specs=[_full_spec(lam_p), _full_spec(g_sub), _per_batch_spec(q),
                  _hist_spec(k_past, tk, n_hist, li), _hist_spec(v_past, tk, n_hist, li),
                  _per_batch_spec(k_new), _per_batch_spec(v_new),
                  pl.BlockSpec((None, DIFF_HEADS, n, tk), lambda b, kj: (bias_index(0, kj), 0, 0, 0)),
                  pl.BlockSpec((None, DIFF_HEADS, n, n), lambda b, kj: (0, 0, 0, 0))],
        out_specs=pl.BlockSpec((None, n, w), lambda b, kj: (b, 0, 0)),
        out_shape=jax.ShapeDtypeStruct((B, n, w), BF16),
        scratch_shapes=[pltpu.VMEM((2 * DIFF_HEADS, n, LANES), F32), pltpu.VMEM((2 * DIFF_HEADS, n, LANES), F32),
                        pltpu.VMEM((2 * DIFF_HEADS, n, DIFF_V), F32)],
        compiler_params=_cparams(("parallel", "arbitrary")),
        name="diff_flash",
    )(lam_p, g_sub, q, k_past, v_past, k_new, v_new, bias_past, bias_new)


def _flash_t(q, k, vt, *, heads, dv, tq, tk, frame, name):
    B, n, qw = q.shape
    nq, nk = n // tq, n // tk

    def last(qi):
        return _last_needed_block(qi, tq, tk, 0, nk, frame)

    cw = min(tq, PROMPT_Q_COLS)
    n_maps = heads * (tq // cw)
    kern = functools.partial(_flash_t_kernel, heads=heads, dv=dv, tq=tq, tk=tk, cw=cw, frame=frame)
    return pl.pallas_call(
        kern,
        grid=(B, nq, nk),
        in_specs=[pl.BlockSpec((None, tq, qw), lambda b, qi, kj: (b, qi, 0)),
                  pl.BlockSpec((None, tk, qw), lambda b, qi, kj: (b, jnp.minimum(kj, last(qi)), 0)),
                  pl.BlockSpec((None, heads * dv, tk), lambda b, qi, kj: (b, 0, jnp.minimum(kj, last(qi))))],
        out_specs=pl.BlockSpec((None, tq, heads * dv), lambda b, qi, kj: (b, qi, 0)),
        out_shape=jax.ShapeDtypeStruct((B, n, heads * dv), BF16),
        scratch_shapes=[pltpu.VMEM((n_maps, 1, cw), F32),
                        pltpu.VMEM((n_maps, dv + BF16_SUBLANES, cw), F32)],
        compiler_params=_cparams(("parallel", "parallel", "arbitrary")),
        name=name,
    )(q, k, vt)


def _diff_flash_t(q, k, vt, bias, lam_p, g_sub, *, tq, tk, lam_init, bias_index):
    B, n, w = q.shape
    nq, nk = n // tq, n // tk

    def last(qi):
        return _last_needed_block(qi, tq, tk, 0, nk, False)

    kern = functools.partial(_diff_flash_t_kernel, tq=tq, tk=tk, lam_init=lam_init)
    return pl.pallas_call(
        kern,
        grid=(B, nq, nk),
        in_specs=[_full_spec(lam_p), _full_spec(g_sub),
                  pl.BlockSpec((None, tq, w), lambda b, qi, kj: (b, qi, 0)),
                  pl.BlockSpec((None, tk, w), lambda b, qi, kj: (b, jnp.minimum(kj, last(qi)), 0)),
                  pl.BlockSpec((None, w, tk), lambda b, qi, kj: (b, 0, jnp.minimum(kj, last(qi)))),
                  pl.BlockSpec((None, DIFF_HEADS, tk, tq), lambda b, qi, kj: (bias_index(qi, kj), 0, 0, 0))],
        out_specs=pl.BlockSpec((None, tq, w), lambda b, qi, kj: (b, qi, 0)),
        out_shape=jax.ShapeDtypeStruct((B, n, w), BF16),
        scratch_shapes=[pltpu.VMEM((2 * DIFF_HEADS, 1, tq), F32),
                        pltpu.VMEM((2 * DIFF_HEADS, DIFF_V + BF16_SUBLANES, tq), F32)],
        compiler_params=_cparams(("parallel", "parallel", "arbitrary")),
        name="diff_flash_t",
    )(lam_p, g_sub, q, k, vt, bias)


def _mix_out(x, oa, ob, oc, gates, w, tm):
    M, D = x.shape
    rows = lambda wd: pl.BlockSpec((tm, wd), lambda t: (t, 0))
    return pl.pallas_call(
        _mix_out_kernel,
        grid=(M // tm,),
        in_specs=[rows(D), rows(MIX_WIDTH), rows(MIX_WIDTH), rows(MIX_WIDTH), rows(N_BRANCHES * D),
                  _full_spec(w["branch"]), _full_spec(w["o"])],
        out_specs=rows(D),
        out_shape=jax.ShapeDtypeStruct((M, D), F32),
        compiler_params=_cparams(("parallel",)),
        name="mix_out",
    )(x, oa, ob, oc, gates, w["branch"], w["o"])


def _ffn(x, w, final_g, tm):
    M, D = x.shape
    rows = pl.BlockSpec((tm, D), lambda t: (t, 0))
    consts = (w["ffn_g"], w["ffn_in"], w["ffn_out"], final_g)
    last = final_g is not None
    if not last:
        consts = consts[:3] + (w["ffn_g"],)
    out_shape = [jax.ShapeDtypeStruct((M, D), F32)] * (2 if last else 1)
    res = pl.pallas_call(
        _ffn_kernel,
        grid=(M // tm,),
        in_specs=[rows] + [_full_spec(a) for a in consts],
        out_specs=[rows] * len(out_shape),
        out_shape=out_shape,
        compiler_params=_cparams(("parallel",)),
        name="ffn_final" if last else "ffn",
    )(x, *consts)
    return res if last else (res[0], None)


def _rot_cols(w):
    half = MLA_ROPE // 2
    return jnp.concatenate([-w[..., half:], w[..., :half]], axis=-1)


def _head_blocks(w, heads, width):
    k = w.shape[0]
    w = w.reshape(k, heads, width)
    return jnp.pad(w, ((0, 0), (0, 0), (0, LANES - width))).reshape(k, heads * LANES)


def _prep_layer_weights(li, p):
    d = p["w_in"].shape[1]
    sizes = (MLA_Q_RANK, MLA_KV_RANK, MLA_ROPE,
             FOX_HEADS * FOX_DIM, FOX_HEADS * FOX_DIM, FOX_HEADS * FOX_DIM, FOX_HEADS,
             DIFF_HEADS * 2 * DIFF_QK, DIFF_HEADS * 2 * DIFF_QK, DIFF_HEADS * DIFF_V, N_BRANCHES * d)
    offs = [0]
    for s in sizes:
        offs.append(offs[-1] + s)
    win = p["w_in"][li]
    col = lambda i: win[:, offs[i]:offs[i + 1]]
    w_cq, w_ckv, w_kr, w_fq, w_fk, w_fv, w_ff, w_dq, w_dk, w_dv, w_gl = (col(i) for i in range(11))

    def rope_block(wr):
        return jnp.pad(wr, ((0, 0), (MLA_NOPE, LANES - MLA_NOPE - MLA_ROPE)))

    mla_in = jnp.concatenate([w_cq, w_ckv, rope_block(w_kr), rope_block(_rot_cols(w_kr))], axis=1)
    uq = p["w_mla_uq"][li].reshape(MLA_Q_RANK, MLA_HEADS, MLA_NOPE + MLA_ROPE)
    uq_nope, uq_rope = uq[..., :MLA_NOPE], uq[..., MLA_NOPE:]
    pad = jnp.zeros((MLA_Q_RANK, MLA_HEADS, LANES - MLA_NOPE - MLA_ROPE), F32)
    uq_cat = jnp.concatenate([uq_nope, uq_rope, pad], axis=-1).reshape(MLA_Q_RANK, MLA_HEADS * LANES)
    uq_rot = jnp.concatenate([jnp.zeros_like(uq_nope), _rot_cols(uq_rope), pad], axis=-1)
    uq2 = jnp.concatenate([uq_cat, uq_rot.reshape(MLA_Q_RANK, MLA_HEADS * LANES)], axis=1)
    ukv = p["w_mla_ukv"][li].reshape(MLA_KV_RANK, MLA_HEADS, MLA_NOPE + MLA_V)
    ukn = _head_blocks(ukv[..., :MLA_NOPE].reshape(MLA_KV_RANK, -1), MLA_HEADS, MLA_NOPE)
    uv = ukv[..., MLA_NOPE:].reshape(MLA_KV_RANK, MLA_HEADS * MLA_V)
    ukv2 = jnp.concatenate([ukn, uv], axis=1)
    fox_in = jnp.concatenate([w_fq, w_fk, w_fv, jnp.pad(w_ff, ((0, 0), (0, LANES - FOX_HEADS)))], axis=1)
    fox_sel = _head_blocks(jnp.eye(FOX_HEADS * FOX_DIM, dtype=F32), FOX_HEADS, FOX_DIM)
    krope_sel = jnp.pad(jnp.eye(MLA_ROPE, dtype=F32), ((0, 0), (MLA_NOPE, LANES - MLA_NOPE - MLA_ROPE)))
    row = lambda a: a.reshape(1, -1)
    return {
        "attn_g": row(p["attn_norm_g"][li]),
        "mla_in": mla_in.astype(BF16),
        "q_g": row(p["mla_q_norm_g"][li]),
        "uq2": uq2.astype(BF16),
        "kv_g": row(p["mla_kv_norm_g"][li]),
        "ukv2": ukv2.astype(BF16),
        "krope_sel": krope_sel.astype(BF16),
        "fox_in": fox_in.astype(BF16),
        "fox_bias": jnp.pad(row(p["fox_f_bias"][li]), ((0, 0), (0, LANES - FOX_HEADS))),
        "fox_sel": fox_sel.astype(BF16),
        "diff_in": jnp.concatenate([w_dq, w_dk, w_dv], axis=1).astype(BF16),
        "gate_in": w_gl.astype(BF16),
        "diff_lambda": p["diff_lambda"][li],
        "diff_g": row(p["diff_subln_g"][li]),
        "branch": p["w_branch"][li].astype(BF16),
        "o": p["w_o"][li].astype(BF16),
        "ffn_g": row(p["ffn_norm_g"][li]),
        "ffn_in": p["w_ffn_in"][li].astype(BF16),
        "ffn_out": p["w_ffn_out"][li].astype(BF16),
    }


def _rope_tables(pos):
    half = MLA_ROPE // 2
    inv = 1.0 / (ROPE_BASE ** (jnp.arange(half, dtype=F32) / half))
    ang = pos.astype(F32)[:, None] * inv[None, :]
    cos, sin = jnp.cos(ang), jnp.sin(ang)
    cos2, sin2 = jnp.concatenate([cos, cos], axis=1), jnp.concatenate([sin, sin], axis=1)
    n = pos.shape[0]
    tail = jnp.zeros((n, LANES - MLA_NOPE - MLA_ROPE), F32)
    scale = (MLA_NOPE + MLA_ROPE) ** -0.5 * LOG2E
    cos_q = jnp.concatenate([jnp.ones((n, MLA_NOPE), F32), cos2, tail], axis=1) * scale
    sin_q = jnp.concatenate([jnp.zeros((n, MLA_NOPE), F32), sin2, tail], axis=1) * scale
    cos_k = jnp.concatenate([jnp.zeros((n, MLA_NOPE), F32), cos2, tail], axis=1)
    sin_k = jnp.concatenate([jnp.zeros((n, MLA_NOPE), F32), sin2, tail], axis=1)
    return cos_q, sin_q, cos_k, sin_k


def _rel_bucket(rel):
    half = REL_BUCKETS // 2
    max_exact = half // 2
    ret = jnp.where(rel > 0, half, 0)
    n = jnp.abs(rel)
    large = max_exact + (jnp.log(jnp.maximum(n, 1).astype(F32) / max_exact)
                         / math.log(REL_MAX_DIST / max_exact) * (half - max_exact)).astype(jnp.int32)
    large = jnp.minimum(large, half - 1)
    return ret + jnp.where(n < max_exact, n, large)


def _bias_tiles(rel_table, tq, tk, deltas, key_major):
    table = rel_table.astype(F32) * LOG2E
    rows, cols = (tk, tq) if key_major else (tq, tk)
    period = rows + cols
    m = jnp.arange(period, dtype=jnp.int32)
    m = jnp.where(m < cols, m, m - period)
    tiles = []
    for d in deltas:
        rel = (-m - d) if key_major else (m - d)
        bucket = _rel_bucket(rel)[None]
        line = jnp.zeros((DIFF_HEADS, period), F32)
        for b in range(REL_BUCKETS):
            line = jnp.where(bucket == b, table[b][:, None], line)
        flat = jnp.tile(line, (1, rows))[:, :rows * (period - 1)]
        tiles.append(flat.reshape(DIFF_HEADS, rows, period - 1)[:, :, :cols])
    return jnp.stack(tiles)


def _bias_plan(n, n_keys, tq, tk, q_off):
    nq, nk = n // tq, n_keys // tk
    step = math.gcd(tq, tk) if nq > 1 else tk
    needed = [q_off + qi * tq - kj * tk for qi in range(nq) for kj in range(nk)
              if ((kj * tk) >> CHUNK_SHIFT) <= ((q_off + qi * tq + tq - 1) >> CHUNK_SHIFT)]
    d_min = min(needed)
    far_from = tk + REL_MAX_DIST - 1
    deltas = list(range(d_min, max(far_from, d_min + 1), step))
    n_near = len(deltas)
    deltas.append(deltas[-1] + step)
    base = (q_off - d_min) // step

    def index(qi, kj):
        return jnp.clip(base + qi * (tq // step) - kj * (tk // step), 0, n_near)

    return deltas, index


def _run_trunk(x, past, weights, rel_table, final_g):
    B, n, D = x.shape
    depth = len(weights)
    n_past = past["mla_latent"].shape[2] if past else 0
    n_keys = n_past + n
    key_major = not past
    if past:
        tq, tk = n, _row_tile(n_past, SAMPLE_KEY_TILE)
        deltas, bias_index = _bias_plan(n, n_past, tq, tk, n_past)
        bias_new = _bias_tiles(rel_table, n, n, [0], False)
    else:
        tq, tk = _row_tile(n, PROMPT_Q_TILE), _row_tile(n, PROMPT_K_TILE)
        deltas, bias_index = _bias_plan(n, n, tq, tk, 0)
    bias = _bias_tiles(rel_table, tq, tk, deltas, key_major)
    tm = _row_tile(n, ROW_TILE)
    pos = n_past + jnp.arange(n, dtype=jnp.int32)
    tabs = _rope_tables(pos)
    zero_carry = jnp.zeros((B, 1, LANES), F32)

    fox_shape = (depth, B, FOX_HEADS, FOX_DIM, n) if key_major else (depth, B, n, FOX_HEADS, FOX_DIM)
    rows = {
        "mla_latent": jnp.zeros((depth, B, n, MLA_KV_RANK), F32),
        "mla_krope": jnp.zeros((depth, B, n, MLA_ROPE), F32),
        "fox_k": jnp.zeros(fox_shape, F32),
        "fox_v": jnp.zeros(fox_shape, F32),
        "fox_logf": jnp.zeros((depth, B, n, FOX_HEADS), F32),
        "diff_k": jnp.zeros((depth, B, n, DIFF_HEADS, 2 * DIFF_QK), F32),
        "diff_v": jnp.zeros((depth, B, n, DIFF_HEADS, DIFF_V), F32),
    }
    y = None
    for li in range(depth):
        w = weights[li]
        q_a, k_a, v_a, rows["mla_latent"], rows["mla_krope"] = _mla_in(
            x, w, tabs, (rows["mla_latent"], rows["mla_krope"]), li, tm, key_major)
        carry = zero_carry
        if past:
            tp = _row_tile(n_past, PAST_ROW_TILE)
            rows_of = lambda name: past[name][li].reshape(B, n_past, -1)
            k_a_p, v_a_p = _mla_past(past["mla_latent"], past["mla_krope"], li, w, tp)
            fk_p, carry = _fox_past(rows_of("fox_k"), past["fox_logf"], li, w, tp)
        q_b, k_b, v_b, _, rows["fox_k"], rows["fox_v"], rows["fox_logf"] = _fox_in(
            x, w, carry, (rows["fox_k"], rows["fox_v"], rows["fox_logf"]), li, tm, key_major)
        q_c, k_c, v_c, gates, rows["diff_k"], rows["diff_v"] = _diff_gate_in(
            x, w, (rows["diff_k"], rows["diff_v"]), li, tm, key_major)
        lam_init = 0.8 - 0.6 * math.exp(-0.3 * li)
        if past:
            o_a = _flash(q_a, k_a_p, v_a_p, k_a, v_a, li=li, heads=MLA_HEADS, dv=MLA_V, tk=tk, frame=False,
                         name="mla_flash")
            o_b = _flash(q_b, fk_p, rows_of("fox_v"), k_b, v_b, li=li, heads=FOX_HEADS, dv=FOX_DIM, tk=tk, frame=True,
                         name="fox_flash")
            o_c = _diff_flash(q_c, rows_of("diff_k"), rows_of("diff_v"), k_c, v_c, bias, bias_new,
                              w["diff_lambda"], w["diff_g"], li=li, tk=tk, lam_init=lam_init, bias_index=bias_index)
        else:
            o_a = _flash_t(q_a, k_a, v_a, heads=MLA_HEADS, dv=MLA_V, tq=tq, tk=tk, frame=False, name="mla_flash_t")
            o_b = _flash_t(q_b, k_b, v_b, heads=FOX_HEADS, dv=FOX_DIM, tq=tq, tk=tk, frame=True, name="fox_flash_t")
            o_c = _diff_flash_t(q_c, k_c, v_c, bias, w["diff_lambda"], w["diff_g"], tq=tq, tk=tk,
                                lam_init=lam_init, bias_index=bias_index)
        M = B * n
        flat = lambda a: a.reshape(M, a.shape[-1])
        x1 = _mix_out(flat(x), flat(o_a), flat(o_b), flat(o_c), flat(gates), w, _row_tile(M, MIX_ROW_TILE))
        x2, y = _ffn(x1, w, final_g if li == depth - 1 else None, _row_tile(M, ROW_TILE))
        x = x2.reshape(B, n, D)
    if key_major:
        for name in ("fox_k", "fox_v"):
            rows[name] = jnp.transpose(rows[name], (0, 1, 4, 2, 3))
    return y.reshape(B, n, D), rows


def kernel(x_prompt, x_sample, cache_mla_latent, cache_mla_krope, cache_fox_k, cache_fox_v, cache_fox_logf,
           cache_diff_k, cache_diff_v, attn_norm_g, w_in, mla_q_norm_g, mla_kv_norm_g, w_mla_uq, w_mla_ukv,
           fox_f_bias, diff_lambda, diff_subln_g, w_branch, w_o, ffn_norm_g, w_ffn_in, w_ffn_out,
           rel_bias_table, final_norm_g):
    params = {
        "attn_norm_g": attn_norm_g, "w_in": w_in, "mla_q_norm_g": mla_q_norm_g,
        "mla_kv_norm_g": mla_kv_norm_g, "w_mla_uq": w_mla_uq, "w_mla_ukv": w_mla_ukv,
        "fox_f_bias": fox_f_bias, "diff_lambda": diff_lambda, "diff_subln_g": diff_subln_g,
        "w_branch": w_branch, "w_o": w_o, "ffn_norm_g": ffn_norm_g, "w_ffn_in": w_ffn_in,
        "w_ffn_out": w_ffn_out,
    }
    past = {
        "mla_latent": cache_mla_latent, "mla_krope": cache_mla_krope, "fox_k": cache_fox_k,
        "fox_v": cache_fox_v, "fox_logf": cache_fox_logf, "diff_k": cache_diff_k, "diff_v": cache_diff_v,
    }
    depth = w_in.shape[0]
    weights = [_prep_layer_weights(li, params) for li in range(depth)]
    final_g = final_norm_g.reshape(1, -1)
    y_p, sp = _run_trunk(x_prompt, {}, weights, rel_bias_table, final_g)
    y_s, ss = _run_trunk(x_sample, past, weights, rel_bias_table, final_g)
    return (y_p, y_s,
            sp["mla_latent"], ss["mla_latent"], sp["mla_krope"], ss["mla_krope"],
            sp["fox_k"], ss["fox_k"], sp["fox_v"], ss["fox_v"],
            sp["fox_logf"], ss["fox_logf"], sp["diff_k"], ss["diff_k"],
            sp["diff_v"], ss["diff_v"])
```

```python
import functools
import math

import jax
import jax.numpy as jnp
from jax import lax
from jax.experimental import pallas as pl
from jax.experimental.pallas import tpu as pltpu

F32 = jnp.float32
BF16 = jnp.bfloat16

CHUNK_SHIFT = 6
NORM_EPS = 1e-6
LOG2E = math.log2(math.e)
NEG_INF = -1e30
MLA_HEADS, MLA_NOPE, MLA_ROPE, MLA_V = 8, 64, 32, 64
MLA_Q_RANK, MLA_KV_RANK = 384, 256
ROPE_BASE = 10000.0
FOX_HEADS, FOX_DIM = 8, 64
DIFF_HEADS, DIFF_QK, DIFF_V = 4, 64, 128
MIX_WIDTH = 512
N_BRANCHES = 3
REL_BUCKETS, REL_MAX_DIST = 32, 128

LANES = 128
VMEM_LIMIT = 56 * 1024 * 1024


def _cparams(sem):
    return pltpu.CompilerParams(dimension_semantics=sem, vmem_limit_bytes=VMEM_LIMIT)


def _rms(x, g):
    return x * lax.rsqrt(jnp.mean(x * x, axis=-1, keepdims=True) + NORM_EPS) * g


def _dot(a, b):
    return jnp.dot(a, b, preferred_element_type=F32)


def _dot_nt(a, b):
    return lax.dot_general(a, b, (((1,), (1,)), ((), ())), preferred_element_type=F32)


def _bf16_part(x):
    return x.astype(BF16).astype(F32)


def _store_v(v_out, v, transposed):
    v_out[...] = (jnp.transpose(v) if transposed else v).astype(BF16)


def _cumsum_rows(x):
    rows = x.shape[0]
    row = lax.broadcasted_iota(jnp.int32, x.shape, 0)
    shift = 1
    while shift < rows:
        x = x + jnp.where(row >= shift, pltpu.roll(x, shift, 0), 0.0)
        shift *= 2
    return x


def _mla_in_kernel(x_ref, ga_ref, win_ref, gq_ref, wuq_ref, gkv_ref, wukv_ref,
                   cq_ref, sq_ref, ck_ref, sk_ref, lat_stack, kr_stack,
                   q_out, k_out, v_out, lat_out, kr_out, *, v_transposed):
    del lat_stack, kr_stack
    hb = _rms(x_ref[...], ga_ref[...]).astype(BF16)
    pm = _dot(hb, win_ref[...])
    cqn = _rms(pm[:, :MLA_Q_RANK], gq_ref[...]).astype(BF16)
    q2 = _dot(cqn, wuq_ref[...])
    hw = MLA_HEADS * LANES
    cosq = jnp.concatenate([cq_ref[...]] * MLA_HEADS, axis=1)
    sinq = jnp.concatenate([sq_ref[...]] * MLA_HEADS, axis=1)
    q_out[...] = (q2[:, :hw] * cosq + q2[:, hw:] * sinq).astype(BF16)
    latent = _rms(pm[:, MLA_Q_RANK:MLA_Q_RANK + MLA_KV_RANK], gkv_ref[...])
    lat_out[...] = latent
    kv2 = _dot(latent.astype(BF16), wukv_ref[...])
    o = MLA_Q_RANK + MLA_KV_RANK
    krb = pm[:, o:o + LANES] * ck_ref[...] + pm[:, o + LANES:o + 2 * LANES] * sk_ref[...]
    kr_out[...] = krb[:, MLA_NOPE:MLA_NOPE + MLA_ROPE]
    k_out[...] = (kv2[:, :hw] + jnp.concatenate([krb] * MLA_HEADS, axis=1)).astype(BF16)
    _store_v(v_out, kv2[:, hw:], v_transposed)


def _decay_columns(c, rows):
    hi = _bf16_part(c)
    r1 = c - hi
    mid = _bf16_part(r1)
    lo = _bf16_part(r1 - mid)
    lane = lax.broadcasted_iota(jnp.int32, (rows, LANES), 1)
    d = FOX_DIM
    ones_q = jnp.where((lane >= d + 3) & (lane < d + 6), 1.0, 0.0)
    ones_k = jnp.where((lane >= d) & (lane < d + 3), 1.0, 0.0)
    eq, ek = [], []
    for h in range(FOX_HEADS):
        bh = jnp.broadcast_to(hi[:, h:h + 1], (rows, LANES))
        bm = jnp.broadcast_to(mid[:, h:h + 1], (rows, LANES))
        bl = jnp.broadcast_to(lo[:, h:h + 1], (rows, LANES))
        eq.append(jnp.where(lane == d, bh, jnp.where(lane == d + 1, bm, jnp.where(lane == d + 2, bl, ones_q))))
        ek.append(jnp.where(lane == d + 3, -bh, jnp.where(lane == d + 4, -bm, jnp.where(lane == d + 5, -bl, ones_k))))
    return jnp.concatenate(eq, axis=1), jnp.concatenate(ek, axis=1)


def _store_heads(out_ref, x, heads):
    width = x.shape[1] // heads
    for h in range(heads):
        out_ref[:, h, :] = x[:, h * width:(h + 1) * width]


def _fox_in_kernel(x_ref, ga_ref, win_ref, fb_ref, sel_ref, c0_ref, kf_stack, vf_stack, lf_stack,
                   q_out, k_out, vb_out, ct_out, kf_out, vf_out, lf_out, carry_sc, *, v_transposed):
    del kf_stack, vf_stack, lf_stack
    rows = x_ref.shape[0]
    t = pl.program_id(1)

    @pl.when(t == 0)
    def _():
        carry_sc[...] = c0_ref[...]

    hb = _rms(x_ref[...], ga_ref[...]).astype(BF16)
    pf = _dot(hb, win_ref[...])
    w = FOX_HEADS * FOX_DIM
    fq = pf[:, :w] * (FOX_DIM ** -0.5 * LOG2E)
    fk = pf[:, w:2 * w]
    fv = pf[:, 2 * w:3 * w]
    z = pf[:, 3 * w:3 * w + LANES] + fb_ref[...]
    lane = lax.broadcasted_iota(jnp.int32, (rows, LANES), 1)
    logf = jnp.minimum(z, 0.0) - jnp.log1p(jnp.exp(-jnp.abs(z)))
    logf = jnp.where(lane < FOX_HEADS, logf, 0.0)
    lf_out[...] = logf[:, :FOX_HEADS]
    c = carry_sc[...] + _cumsum_rows(logf)
    carry_sc[...] = c[rows - 1:rows, :]
    ct_out[...] = c[rows - 1:rows, :]
    if v_transposed:
        fk_t, fv_t = jnp.transpose(fk), jnp.transpose(fv)
        for h in range(FOX_HEADS):
            kf_out[h] = fk_t[h * FOX_DIM:(h + 1) * FOX_DIM, :]
            vf_out[h] = fv_t[h * FOX_DIM:(h + 1) * FOX_DIM, :]
        vb_out[...] = fv_t.astype(BF16)
    else:
        _store_heads(kf_out, fk, FOX_HEADS)
        _store_heads(vf_out, fv, FOX_HEADS)
        vb_out[...] = fv.astype(BF16)
    eq, ek = _decay_columns(c * LOG2E, rows)
    q_out[...] = (_dot(fq.astype(BF16), sel_ref[...]) + eq).astype(BF16)
    k_out[...] = (_dot(fk.astype(BF16), sel_ref[...]) + ek).astype(BF16)


def _diff_gate_in_kernel(x_ref, ga_ref, wd_ref, wg_ref, dk_stack, dv_stack,
                         dq_out, dkb_out, dvb_out, gate_out, dk_out, dv_out, *, v_transposed):
    del dk_stack, dv_stack
    hb = _rms(x_ref[...], ga_ref[...]).astype(BF16)
    pd = _dot(hb, wd_ref[...])
    w = DIFF_HEADS * 2 * DIFF_QK
    dq_out[...] = (pd[:, :w] * (DIFF_QK ** -0.5 * LOG2E)).astype(BF16)
    dk = pd[:, w:2 * w]
    dv = pd[:, 2 * w:]
    _store_heads(dk_out, dk, DIFF_HEADS)
    _store_heads(dv_out, dv, DIFF_HEADS)
    dkb_out[...] = dk.astype(BF16)
    _store_v(dvb_out, dv, v_transposed)
    gate_out[...] = jax.nn.sigmoid(_dot(hb, wg_ref[...]))


def _mla_past_kernel(lat_ref, kr_ref, wukv_ref, selk_ref, k_out, v_out):
    kv2 = _dot(lat_ref[...].astype(BF16), wukv_ref[...])
    krb = _dot(kr_ref[...].astype(BF16), selk_ref[...])
    hw = MLA_HEADS * LANES
    k_out[...] = (kv2[:, :hw] + jnp.concatenate([krb] * MLA_HEADS, axis=1)).astype(BF16)
    v_out[...] = kv2[:, hw:].astype(BF16)


def _fox_past_kernel(k_ref, lf_ref, sel_ref, k_out, ct_out, carry_sc, lf_sc):
    rows = k_ref.shape[0]
    t = pl.program_id(1)

    @pl.when(t == 0)
    def _():
        carry_sc[...] = jnp.zeros(carry_sc.shape, F32)

    lf_sc[...] = jnp.zeros(lf_sc.shape, F32)
    lf_sc[:, :FOX_HEADS] = lf_ref[...]
    c = carry_sc[...] + _cumsum_rows(lf_sc[...])
    carry_sc[...] = c[rows - 1:rows, :]
    ct_out[...] = c[rows - 1:rows, :]
    _, ek = _decay_columns(c * LOG2E, rows)
    k_out[...] = (_dot(k_ref[...].astype(BF16), sel_ref[...]) + ek).astype(BF16)


def _block_visibility(q0, k0, tq, tk, frame):
    if frame:
        return k0 <= q0 + (tq - 1), k0 + (tk - 1) <= q0
    return ((k0 >> CHUNK_SHIFT) <= ((q0 + (tq - 1)) >> CHUNK_SHIFT),
            ((k0 + (tk - 1)) >> CHUNK_SHIFT) <= (q0 >> CHUNK_SHIFT))


def _mask(q0, k0, tq, tk, frame):
    row = q0 + lax.broadcasted_iota(jnp.int32, (tq, tk), 0)
    col = k0 + lax.broadcasted_iota(jnp.int32, (tq, tk), 1)
    if frame:
        return col <= row
    return (col >> CHUNK_SHIFT) <= (row >> CHUNK_SHIFT)


def _online_softmax_step(s, v, idx, m_sc, l_sc, acc_sc):
    dv = v.shape[1]
    m_prev = m_sc[idx]
    m_new = jnp.maximum(m_prev, jnp.max(s, axis=1, keepdims=True))
    alpha = jnp.exp2(m_prev - m_new)
    p = jnp.exp2(s - m_new[:, :1])
    l_sc[idx] = alpha * l_sc[idx] + jnp.sum(p, axis=1, keepdims=True)
    m_sc[idx] = m_new
    acc_sc[idx] = acc_sc[idx] * alpha[:, :dv] + _dot(p.astype(BF16), v)


def _head_cols(ref, h, width):
    if len(ref.shape) == 3:
        return ref[:, h, :]
    return ref[:, h * width:(h + 1) * width]


def _init_softmax_state(m_sc, l_sc, acc_sc):
    m_sc[...] = jnp.full(m_sc.shape, NEG_INF, F32)
    l_sc[...] = jnp.zeros(l_sc.shape, F32)
    acc_sc[...] = jnp.zeros(acc_sc.shape, F32)


def _flash_kernel(q_ref, kp_ref, vp_ref, kn_ref, vn_ref, o_ref, m_sc, l_sc, acc_sc, *, heads, dv, n_past, frame):
    kj = pl.program_id(1)
    n_hist = pl.num_programs(1) - 1
    tq = q_ref.shape[0]

    @pl.when(kj == 0)
    def _():
        _init_softmax_state(m_sc, l_sc, acc_sc)

    def step(k_ref, v_ref, mask):
        scores = lambda h: _dot_nt(q_ref[:, h * LANES:(h + 1) * LANES],
                                   k_ref[:, h * LANES:(h + 1) * LANES].astype(BF16))
        s_next = scores(0)
        for h in range(heads):
            s = s_next
            if h + 1 < heads:
                s_next = scores(h + 1)
            if mask is not None:
                s = jnp.where(mask, s, NEG_INF)
            _online_softmax_step(s, _head_cols(v_ref, h, dv).astype(BF16), h, m_sc, l_sc, acc_sc)

    @pl.when(kj < n_hist)
    def _():
        step(kp_ref, vp_ref, None)

    @pl.when(kj == n_hist)
    def _():
        step(kn_ref, vn_ref, _mask(n_past, n_past, tq, kn_ref.shape[0], frame))
        for h in range(heads):
            o_ref[:, h * dv:(h + 1) * dv] = (acc_sc[h] / l_sc[h][:, :dv]).astype(o_ref.dtype)


def _diff_flash_kernel(lam_ref, g_ref, q_ref, kp_ref, vp_ref, kn_ref, vn_ref, bp_ref, bn_ref, o_ref,
                       m_sc, l_sc, acc_sc, *, n_past, lam_init):
    kj = pl.program_id(1)
    n_hist = pl.num_programs(1) - 1
    tq = q_ref.shape[0]

    @pl.when(kj == 0)
    def _():
        _init_softmax_state(m_sc, l_sc, acc_sc)

    def step(k_ref, v_ref, b_ref, mask):
        lane = lax.broadcasted_iota(jnp.int32, (tq, LANES), 1)

        def scores(i):
            h, c = divmod(i, 2)
            q = q_ref[:, h * LANES:(h + 1) * LANES]
            qc = jnp.where((lane < DIFF_QK) if c == 0 else (lane >= DIFF_QK), q, jnp.zeros_like(q))
            return _dot_nt(qc, _head_cols(k_ref, h, LANES).astype(BF16))

        n_maps = 2 * DIFF_HEADS
        s_next = scores(0)
        for i in range(n_maps):
            s = s_next + b_ref[i // 2]
            if i + 1 < n_maps:
                s_next = scores(i + 1)
            if mask is not None:
                s = jnp.where(mask, s, NEG_INF)
            v = _head_cols(v_ref, i // 2, LANES).astype(BF16)
            _online_softmax_step(s, v, i, m_sc, l_sc, acc_sc)

    @pl.when(kj < n_hist)
    def _():
        step(kp_ref, vp_ref, bp_ref, None)

    @pl.when(kj == n_hist)
    def _():
        step(kn_ref, vn_ref, bn_ref, _mask(n_past, n_past, tq, kn_ref.shape[0], False))
        lp = lam_ref[...]
        lam = (jnp.exp(jnp.sum(lp[0:1, :] * lp[1:2, :], axis=1, keepdims=True))
               - jnp.exp(jnp.sum(lp[2:3, :] * lp[3:4, :], axis=1, keepdims=True)) + lam_init)
        g = g_ref[...]
        for h in range(DIFF_HEADS):
            o = acc_sc[2 * h] / l_sc[2 * h] - lam * (acc_sc[2 * h + 1] / l_sc[2 * h + 1])
            o_ref[:, h * LANES:(h + 1) * LANES] = (_rms(o, g) * (1.0 - lam_init)).astype(o_ref.dtype)


def _mask_t(q0, k0, tq, tk, frame):
    key = k0 + lax.broadcasted_iota(jnp.int32, (tk, tq), 0)
    qry = q0 + lax.broadcasted_iota(jnp.int32, (tk, tq), 1)
    if frame:
        return key <= qry
    return (key >> CHUNK_SHIFT) <= (qry >> CHUNK_SHIFT)


def _softmax_weights_t(st, idx, m_sc):
    m_prev = m_sc[idx]
    m_new = jnp.maximum(m_prev, jnp.max(st, axis=0, keepdims=True))
    m_sc[idx] = m_new
    return jnp.exp2(st - m_new).astype(BF16), jnp.exp2(m_prev - m_new)


def _accumulate_t(p, alpha, vt, idx, acc_sc):
    acc_sc[idx] = acc_sc[idx] * alpha + _dot(vt, p)


def _pipelined_maps_t(n_maps, scores, logits, values, m_sc, acc_sc):
    st_next = scores(0)
    pending = None
    for i in range(n_maps + 1):
        st = st_next
        if i + 1 < n_maps:
            st_next = scores(i + 1)
        current = (_softmax_weights_t(logits(i, st), i, m_sc) + (i,)) if i < n_maps else None
        if pending is not None:
            p, alpha, j = pending
            _accumulate_t(p, alpha, values(j), j, acc_sc)
        pending = current


def _values_with_ones(vt_ref, h, dv, tk):
    return jnp.concatenate([vt_ref[h * dv:(h + 1) * dv, :], jnp.ones((BF16_SUBLANES, tk), BF16)], axis=0)


def _normalised(acc, dv):
    return acc[:dv, :] / acc[dv:dv + 1, :]


def _flash_t_kernel(q_ref, k_ref, vt_ref, o_ref, m_sc, acc_sc, *, heads, dv, tq, tk, cw, frame):
    qi, kj = pl.program_id(1), pl.program_id(2)
    q0 = qi * tq
    k0 = kj * tk
    n_col = tq // cw

    @pl.when(kj == 0)
    def _():
        m_sc[...] = jnp.full(m_sc.shape, NEG_INF, F32)
        acc_sc[...] = jnp.zeros(acc_sc.shape, F32)

    needed, full = _block_visibility(q0, k0, tq, tk, frame)

    def step(masked):
        mask = _mask_t(q0, k0, tq, tk, frame) if masked else None

        def scores(i):
            h, c = divmod(i, n_col)
            return _dot_nt(k_ref[:, h * LANES:(h + 1) * LANES], q_ref[c * cw:(c + 1) * cw, h * LANES:(h + 1) * LANES])

        def logits(i, st):
            c = i % n_col
            return jnp.where(mask[:, c * cw:(c + 1) * cw], st, NEG_INF) if masked else st

        values = lambda i: _values_with_ones(vt_ref, i // n_col, dv, tk)
        _pipelined_maps_t(heads * n_col, scores, logits, values, m_sc, acc_sc)

    @pl.when(needed & full)
    def _():
        step(False)

    @pl.when(needed & jnp.logical_not(full))
    def _():
        step(True)

    @pl.when(kj == pl.num_programs(2) - 1)
    def _():
        per_blk = LANES // dv
        for g in range(heads // per_blk):
            for c in range(n_col):
                rows = [_normalised(acc_sc[(g * per_blk + i) * n_col + c], dv) for i in range(per_blk)]
                blk = rows[0] if per_blk == 1 else jnp.concatenate(rows, axis=0)
                o_ref[c * cw:(c + 1) * cw, g * LANES:(g + 1) * LANES] = jnp.transpose(blk).astype(o_ref.dtype)


def _diff_flash_t_kernel(lam_ref, g_ref, q_ref, k_ref, vt_ref, b_ref, o_ref, m_sc, acc_sc,
                         *, tq, tk, lam_init):
    qi, kj = pl.program_id(1), pl.program_id(2)
    q0 = qi * tq
    k0 = kj * tk

    @pl.when(kj == 0)
    def _():
        m_sc[...] = jnp.full(m_sc.shape, NEG_INF, F32)
        acc_sc[...] = jnp.zeros(acc_sc.shape, F32)

    needed, full = _block_visibility(q0, k0, tq, tk, False)

    def step(masked):
        mask = _mask_t(q0, k0, tq, tk, False) if masked else None
        lane = lax.broadcasted_iota(jnp.int32, (tq, LANES), 1)

        def scores(i):
            h, c = divmod(i, 2)
            q = q_ref[:, h * LANES:(h + 1) * LANES]
            qc = jnp.where((lane < DIFF_QK) if c == 0 else (lane >= DIFF_QK), q, jnp.zeros_like(q))
            return _dot_nt(k_ref[:, h * LANES:(h + 1) * LANES], qc)

        def logits(i, st):
            st = st + b_ref[i // 2]
            return jnp.where(mask, st, NEG_INF) if masked else st

        values = lambda i: _values_with_ones(vt_ref, i // 2, DIFF_V, tk)
        _pipelined_maps_t(2 * DIFF_HEADS, scores, logits, values, m_sc, acc_sc)

    @pl.when(needed & full)
    def _():
        step(False)

    @pl.when(needed & jnp.logical_not(full))
    def _():
        step(True)

    @pl.when(kj == pl.num_programs(2) - 1)
    def _():
        lp = lam_ref[...]
        lam = (jnp.exp(jnp.sum(lp[0:1, :] * lp[1:2, :], axis=1, keepdims=True))
               - jnp.exp(jnp.sum(lp[2:3, :] * lp[3:4, :], axis=1, keepdims=True)) + lam_init)
        g = g_ref[...]
        for h in range(DIFF_HEADS):
            o_t = _normalised(acc_sc[2 * h], DIFF_V) - lam * _normalised(acc_sc[2 * h + 1], DIFF_V)
            o = jnp.transpose(o_t)
            o_ref[:, h * LANES:(h + 1) * LANES] = (_rms(o, g) * (1.0 - lam_init)).astype(o_ref.dtype)


def _mix_out_kernel(x_ref, oa_ref, ob_ref, oc_ref, gate_ref, wb_ref, wo_ref, x_out):
    d = x_ref.shape[1]
    mix = None
    for j, o_ref in enumerate((oa_ref, ob_ref, oc_ref)):
        term = gate_ref[:, j * d:(j + 1) * d] * _dot(o_ref[...], wb_ref[j])
        mix = term if mix is None else mix + term
    x_out[...] = x_ref[...] + _dot(mix.astype(BF16), wo_ref[...])


def _ffn_kernel(x_ref, g_ref, win_ref, wout_ref, gfin_ref, x_out, y_out=None):
    x = x_ref[...]
    gu = _dot(_rms(x, g_ref[...]).astype(BF16), win_ref[...])
    dff = wout_ref.shape[0]
    gate, up = gu[:, :dff], gu[:, dff:]
    x2 = x + _dot((gate * jax.nn.sigmoid(gate) * up).astype(BF16), wout_ref[...])
    x_out[...] = x2
    if y_out is not None:
        y_out[...] = _rms(x2, gfin_ref[...])


BF16_SUBLANES = 16
ROW_TILE = 256
MIX_ROW_TILE = 512
PAST_ROW_TILE = 512
PROMPT_Q_TILE = 1024
PROMPT_K_TILE = 512
PROMPT_Q_COLS = 1024
SAMPLE_KEY_TILE = 2048


def _row_tile(n, target):
    for t in range(min(n, target), 0, -1):
        if n % t == 0 and (t % BF16_SUBLANES == 0 or t == n):
            return t
    raise ValueError((n, target))


def _full_spec(arr):
    nd = arr.ndim
    return pl.BlockSpec(arr.shape, lambda *_: (0,) * nd)


def _rows3(width, tm):
    return pl.BlockSpec((None, tm, width), lambda b, t: (b, t, 0))


def _layer_rows_spec(arr, li, tm):
    tail = arr.shape[3:]
    return pl.BlockSpec((None, None, tm) + tail, lambda b, t: (li, b, t) + (0,) * len(tail))


ALIASED = pl.BlockSpec(memory_space=pl.ANY)


def _work_specs(outs, B, n, tm, v_index, v_t):
    specs = [_rows3(wd, tm) for wd, _ in outs]
    shapes = [jax.ShapeDtypeStruct((B, n, wd), dt) for wd, dt in outs]
    if v_t:
        wd, dt = outs[v_index]
        specs[v_index] = pl.BlockSpec((None, wd, tm), lambda b, t: (b, 0, t))
        shapes[v_index] = jax.ShapeDtypeStruct((B, wd, n), dt)
    return specs, shapes


def _layer_cols_spec(arr, li, tm):
    mid = arr.shape[2:-1]
    return pl.BlockSpec((None, None) + mid + (tm,), lambda b, t: (li, b) + (0,) * len(mid) + (t,))


def _stacked_outputs(stacks, li, tm, n_inputs, n_work_outputs, feature_major=()):
    specs = [(_layer_cols_spec if i in feature_major else _layer_rows_spec)(a, li, tm) for i, a in enumerate(stacks)]
    shapes = [jax.ShapeDtypeStruct(a.shape, a.dtype) for a in stacks]
    aliases = {n_inputs + i: n_work_outputs + i for i in range(len(stacks))}
    return specs, shapes, aliases


def _mla_in(x, w, tabs, stacks, li, tm, v_t):
    B, n, D = x.shape
    hw = MLA_HEADS * LANES
    tab_spec = pl.BlockSpec((tm, LANES), lambda b, t: (t, 0))
    consts = (w["attn_g"], w["mla_in"], w["q_g"], w["uq2"], w["kv_g"], w["ukv2"])
    outs = [(hw, BF16), (hw, BF16), (MLA_HEADS * MLA_V, BF16)]
    specs, shapes = _work_specs(outs, B, n, tm, 2, v_t)
    n_in = 1 + len(consts) + len(tabs)
    s_specs, s_shapes, aliases = _stacked_outputs(stacks, li, tm, n_in, len(outs))
    return pl.pallas_call(
        functools.partial(_mla_in_kernel, v_transposed=v_t),
        grid=(B, n // tm),
        in_specs=[_rows3(D, tm)] + [_full_spec(a) for a in consts] + [tab_spec] * 4 + [ALIASED] * len(stacks),
        out_specs=specs + s_specs,
        out_shape=shapes + s_shapes,
        input_output_aliases=aliases,
        compiler_params=_cparams(("parallel", "parallel")),
        name="mla_in",
    )(x, *consts, *tabs, *stacks)


def _fox_in(x, w, c0, stacks, li, tm, v_t):
    B, n, D = x.shape
    hw = FOX_HEADS * LANES
    wd = FOX_HEADS * FOX_DIM
    consts = (w["attn_g"], w["fox_in"], w["fox_bias"], w["fox_sel"])
    row1 = pl.BlockSpec((None, 1, LANES), lambda b, t: (b, 0, 0))
    outs = [(hw, BF16), (hw, BF16), (wd, BF16)]
    specs, shapes = _work_specs(outs, B, n, tm, 2, v_t)
    n_in = 1 + len(consts) + 1
    s_specs, s_shapes, aliases = _stacked_outputs(stacks, li, tm, n_in, len(outs) + 1,
                                                  feature_major=(0, 1) if v_t else ())
    return pl.pallas_call(
        functools.partial(_fox_in_kernel, v_transposed=v_t),
        grid=(B, n // tm),
        in_specs=[_rows3(D, tm)] + [_full_spec(a) for a in consts] + [row1] + [ALIASED] * len(stacks),
        out_specs=specs + [row1] + s_specs,
        out_shape=shapes + [jax.ShapeDtypeStruct((B, 1, LANES), F32)] + s_shapes,
        input_output_aliases=aliases,
        scratch_shapes=[pltpu.VMEM((1, LANES), F32)],
        compiler_params=_cparams(("arbitrary", "arbitrary")),
        name="fox_in",
    )(x, *consts, c0, *stacks)


def _diff_gate_in(x, w, stacks, li, tm, v_t):
    B, n, D = x.shape
    wd = DIFF_HEADS * DIFF_V
    consts = (w["attn_g"], w["diff_in"], w["gate_in"])
    outs = [(wd, BF16), (wd, BF16), (wd, BF16), (N_BRANCHES * D, F32)]
    specs, shapes = _work_specs(outs, B, n, tm, 2, v_t)
    n_in = 1 + len(consts)
    s_specs, s_shapes, aliases = _stacked_outputs(stacks, li, tm, n_in, len(outs))
    return pl.pallas_call(
        functools.partial(_diff_gate_in_kernel, v_transposed=v_t),
        grid=(B, n // tm),
        in_specs=[_rows3(D, tm)] + [_full_spec(a) for a in consts] + [ALIASED] * len(stacks),
        out_specs=specs + s_specs,
        out_shape=shapes + s_shapes,
        input_output_aliases=aliases,
        compiler_params=_cparams(("parallel", "parallel")),
        name="diff_gate_in",
    )(x, *consts, *stacks)


def _mla_past(lat, kr, li, w, tm):
    _, B, n, _ = lat.shape
    hw = MLA_HEADS * LANES
    consts = (w["ukv2"], w["krope_sel"])
    return pl.pallas_call(
        _mla_past_kernel,
        grid=(B, n // tm),
        in_specs=[_layer_rows_spec(lat, li, tm), _layer_rows_spec(kr, li, tm)] + [_full_spec(a) for a in consts],
        out_specs=[_rows3(hw, tm), _rows3(MLA_HEADS * MLA_V, tm)],
        out_shape=[jax.ShapeDtypeStruct((B, n, hw), BF16), jax.ShapeDtypeStruct((B, n, MLA_HEADS * MLA_V), BF16)],
        compiler_params=_cparams(("parallel", "parallel")),
        name="mla_past",
    )(lat, kr, *consts)


def _fox_past(k, lf, li, w, tm):
    _, B, n, _ = k.shape
    hw = FOX_HEADS * LANES
    row1 = pl.BlockSpec((None, 1, LANES), lambda b, t: (b, 0, 0))
    return pl.pallas_call(
        _fox_past_kernel,
        grid=(B, n // tm),
        in_specs=[_layer_rows_spec(k, li, tm), _layer_rows_spec(lf, li, tm), _full_spec(w["fox_sel"])],
        out_specs=[_rows3(hw, tm), row1],
        out_shape=[jax.ShapeDtypeStruct((B, n, hw), BF16), jax.ShapeDtypeStruct((B, 1, LANES), F32)],
        scratch_shapes=[pltpu.VMEM((1, LANES), F32), pltpu.VMEM((tm, LANES), F32)],
        compiler_params=_cparams(("arbitrary", "arbitrary")),
        name="fox_past",
    )(k, lf, w["fox_sel"])


def _last_needed_block(qi, tq, tk, q_off, nk, frame):
    last_pos = q_off + qi * tq + (tq - 1)
    if not frame:
        last_pos = ((last_pos >> CHUNK_SHIFT) << CHUNK_SHIFT) + ((1 << CHUNK_SHIFT) - 1)
    return jnp.minimum(last_pos // tk, nk - 1)


def _hist_spec(arr, tk, n_hist, li):
    if arr.ndim == 3:
        return pl.BlockSpec((None, tk, arr.shape[2]), lambda b, kj: (b, jnp.minimum(kj, n_hist - 1), 0))
    tail = arr.shape[3:]
    return pl.BlockSpec((None, None, tk) + tail,
                        lambda b, kj: (li, b, jnp.minimum(kj, n_hist - 1)) + (0,) * len(tail))


def _hist_rows(arr):
    return arr.shape[1] if arr.ndim == 3 else arr.shape[2]


def _per_batch_spec(arr):
    return pl.BlockSpec((None,) + arr.shape[1:], lambda b, kj: (b, 0, 0))


def _flash(q, k_past, v_past, k_new, v_new, *, li, heads, dv, tk, frame, name):
    B, n, _ = q.shape
    n_past = _hist_rows(k_past)
    n_hist = n_past // tk
    kern = functools.partial(_flash_kernel, heads=heads, dv=dv, n_past=n_past, frame=frame)
    return pl.pallas_call(
        kern,
        grid=(B, n_hist + 1),
        in_specs=[_per_batch_spec(q), _hist_spec(k_past, tk, n_hist, li), _hist_spec(v_past, tk, n_hist, li),
                  _per_batch_spec(k_new), _per_batch_spec(v_new)],
        out_specs=pl.BlockSpec((None, n, heads * dv), lambda b, kj: (b, 0, 0)),
        out_shape=jax.ShapeDtypeStruct((B, n, heads * dv), BF16),
        scratch_shapes=[pltpu.VMEM((heads, n, LANES), F32), pltpu.VMEM((heads, n, LANES), F32),
                        pltpu.VMEM((heads, n, dv), F32)],
        compiler_params=_cparams(("parallel", "arbitrary")),
        name=name,
    )(q, k_past, v_past, k_new, v_new)


def _diff_flash(q, k_past, v_past, k_new, v_new, bias_past, bias_new, lam_p, g_sub, *, li, tk, lam_init,
                bias_index):
    B, n, w = q.shape
    n_past = _hist_rows(k_past)
    n_hist = n_past // tk
    kern = functools.partial(_diff_flash_kernel, n_past=n_past, lam_init=lam_init)
    return pl.pallas_call(
        kern,
        grid=(B, n_hist + 1),
        in_specs=[_full_spec(lam_p), _full_spec(g_sub), _per_batch_spec(q),
                  _hist_spec(k_past, tk, n_hist, li), _hist_spec(v_past, tk, n_hist, li),
                  _per_batch_spec(k_new), _per_batch_spec(v_new),
                  pl.BlockSpec((None, DIFF_HEADS, n, tk), lambda b, kj: (bias_index(0, kj), 0, 0, 0)),
                  pl.BlockSpec((None, DIFF_HEADS, n, n), lambda b, kj: (0, 0, 0, 0))],
        out_specs=pl.BlockSpec((None, n, w), lambda b, kj: (b, 0, 0)),
        out_shape=jax.ShapeDtypeStruct((B, n, w), BF16),
        scratch_shapes=[pltpu.VMEM((2 * DIFF_HEADS, n, LANES), F32), pltpu.VMEM((2 * DIFF_HEADS, n, LANES), F32),
                        pltpu.VMEM((2 * DIFF_HEADS, n, DIFF_V), F32)],
        compiler_params=_cparams(("parallel", "arbitrary")),
        name="diff_flash",
    )(lam_p, g_sub, q, k_past, v_past, k_new, v_new, bias_past, bias_new)


def _flash_t(q, k, vt, *, heads, dv, tq, tk, frame, name):
    B, n, qw = q.shape
    nq, nk = n // tq, n // tk

    def last(qi):
        return _last_needed_block(qi, tq, tk, 0, nk, frame)

    cw = min(tq, PROMPT_Q_COLS)
    n_maps = heads * (tq // cw)
    kern = functools.partial(_flash_t_kernel, heads=heads, dv=dv, tq=tq, tk=tk, cw=cw, frame=frame)
    return pl.pallas_call(
        kern,
        grid=(B, nq, nk),
        in_specs=[pl.BlockSpec((None, tq, qw), lambda b, qi, kj: (b, qi, 0)),
                  pl.BlockSpec((None, tk, qw), lambda b, qi, kj: (b, jnp.minimum(kj, last(qi)), 0)),
                  pl.BlockSpec((None, heads * dv, tk), lambda b, qi, kj: (b, 0, jnp.minimum(kj, last(qi))))],
        out_specs=pl.BlockSpec((None, tq, heads * dv), lambda b, qi, kj: (b, qi, 0)),
        out_shape=jax.ShapeDtypeStruct((B, n, heads * dv), BF16),
        scratch_shapes=[pltpu.VMEM((n_maps, 1, cw), F32),
                        pltpu.VMEM((n_maps, dv + BF16_SUBLANES, cw), F32)],
        compiler_params=_cparams(("parallel", "parallel", "arbitrary")),
        name=name,
    )(q, k, vt)


def _diff_flash_t(q, k, vt, bias, lam_p, g_sub, *, tq, tk, lam_init, bias_index):
    B, n, w = q.shape
    nq, nk = n // tq, n // tk

    def last(qi):
        return _last_needed_block(qi, tq, tk, 0, nk, False)

    kern = functools.partial(_diff_flash_t_kernel, tq=tq, tk=tk, lam_init=lam_init)
    return pl.pallas_call(
        kern,
        grid=(B, nq, nk),
        in_specs=[_full_spec(lam_p), _full_spec(g_sub),
                  pl.BlockSpec((None, tq, w), lambda b, qi, kj: (b, qi, 0)),
                  pl.BlockSpec((None, tk, w), lambda b, qi, kj: (b, jnp.minimum(kj, last(qi)), 0)),
                  pl.BlockSpec((None, w, tk), lambda b, qi, kj: (b, 0, jnp.minimum(kj, last(qi)))),
                  pl.BlockSpec((None, DIFF_HEADS, tk, tq), lambda b, qi, kj: (bias_index(qi, kj), 0, 0, 0))],
        out_specs=pl.BlockSpec((None, tq, w), lambda b, qi, kj: (b, qi, 0)),
        out_shape=jax.ShapeDtypeStruct((B, n, w), BF16),
        scratch_shapes=[pltpu.VMEM((2 * DIFF_HEADS, 1, tq), F32),
                        pltpu.VMEM((2 * DIFF_HEADS, DIFF_V + BF16_SUBLANES, tq), F32)],
        compiler_params=_cparams(("parallel", "parallel", "arbitrary")),
        name="diff_flash_t",
    )(lam_p, g_sub, q, k, vt, bias)


def _mix_out(x, oa, ob, oc, gates, w, tm):
    M, D = x.shape
    rows = lambda wd: pl.BlockSpec((tm, wd), lambda t: (t, 0))
    return pl.pallas_call(
        _mix_out_kernel,
        grid=(M // tm,),
        in_specs=[rows(D), rows(MIX_WIDTH), rows(MIX_WIDTH), rows(MIX_WIDTH), rows(N_BRANCHES * D),
                  _full_spec(w["branch"]), _full_spec(w["o"])],
        out_specs=rows(D),
        out_shape=jax.ShapeDtypeStruct((M, D), F32),
        compiler_params=_cparams(("parallel",)),
        name="mix_out",
    )(x, oa, ob, oc, gates, w["branch"], w["o"])


def _ffn(x, w, final_g, tm):
    M, D = x.shape
    rows = pl.BlockSpec((tm, D), lambda t: (t, 0))
    consts = (w["ffn_g"], w["ffn_in"], w["ffn_out"], final_g)
    last = final_g is not None
    if not last:
        consts = consts[:3] + (w["ffn_g"],)
    out_shape = [jax.ShapeDtypeStruct((M, D), F32)] * (2 if last else 1)
    res = pl.pallas_call(
        _ffn_kernel,
        grid=(M // tm,),
        in_specs=[rows] + [_full_spec(a) for a in consts],
        out_specs=[rows] * len(out_shape),
        out_shape=out_shape,
        compiler_params=_cparams(("parallel",)),
        name="ffn_final" if last else "ffn",
    )(x, *consts)
    return res if last else (res[0], None)


def _rot_cols(w):
    half = MLA_ROPE // 2
    return jnp.concatenate([-w[..., half:], w[..., :half]], axis=-1)


def _head_blocks(w, heads, width):
    k = w.shape[0]
    w = w.reshape(k, heads, width)
    return jnp.pad(w, ((0, 0), (0, 0), (0, LANES - width))).reshape(k, heads * LANES)


def _prep_layer_weights(li, p):
    d = p["w_in"].shape[1]
    sizes = (MLA_Q_RANK, MLA_KV_RANK, MLA_ROPE,
             FOX_HEADS * FOX_DIM, FOX_HEADS * FOX_DIM, FOX_HEADS * FOX_DIM, FOX_HEADS,
             DIFF_HEADS * 2 * DIFF_QK, DIFF_HEADS * 2 * DIFF_QK, DIFF_HEADS * DIFF_V, N_BRANCHES * d)
    offs = [0]
    for s in sizes:
        offs.append(offs[-1] + s)
    win = p["w_in"][li]
    col = lambda i: win[:, offs[i]:offs[i + 1]]
    w_cq, w_ckv, w_kr, w_fq, w_fk, w_fv, w_ff, w_dq, w_dk, w_dv, w_gl = (col(i) for i in range(11))

    def rope_block(wr):
        return jnp.pad(wr, ((0, 0), (MLA_NOPE, LANES - MLA_NOPE - MLA_ROPE)))

    mla_in = jnp.concatenate([w_cq, w_ckv, rope_block(w_kr), rope_block(_rot_cols(w_kr))], axis=1)
    uq = p["w_mla_uq"][li].reshape(MLA_Q_RANK, MLA_HEADS, MLA_NOPE + MLA_ROPE)
    uq_nope, uq_rope = uq[..., :MLA_NOPE], uq[..., MLA_NOPE:]
    pad = jnp.zeros((MLA_Q_RANK, MLA_HEADS, LANES - MLA_NOPE - MLA_ROPE), F32)
    uq_cat = jnp.concatenate([uq_nope, uq_rope, pad], axis=-1).reshape(MLA_Q_RANK, MLA_HEADS * LANES)
    uq_rot = jnp.concatenate([jnp.zeros_like(uq_nope), _rot_cols(uq_rope), pad], axis=-1)
    uq2 = jnp.concatenate([uq_cat, uq_rot.reshape(MLA_Q_RANK, MLA_HEADS * LANES)], axis=1)
    ukv = p["w_mla_ukv"][li].reshape(MLA_KV_RANK, MLA_HEADS, MLA_NOPE + MLA_V)
    ukn = _head_blocks(ukv[..., :MLA_NOPE].reshape(MLA_KV_RANK, -1), MLA_HEADS, MLA_NOPE)
    uv = ukv[..., MLA_NOPE:].reshape(MLA_KV_RANK, MLA_HEADS * MLA_V)
    ukv2 = jnp.concatenate([ukn, uv], axis=1)
    fox_in = jnp.concatenate([w_fq, w_fk, w_fv, jnp.pad(w_ff, ((0, 0), (0, LANES - FOX_HEADS)))], axis=1)
    fox_sel = _head_blocks(jnp.eye(FOX_HEADS * FOX_DIM, dtype=F32), FOX_HEADS, FOX_DIM)
    krope_sel = jnp.pad(jnp.eye(MLA_ROPE, dtype=F32), ((0, 0), (MLA_NOPE, LANES - MLA_NOPE - MLA_ROPE)))
    row = lambda a: a.reshape(1, -1)
    return {
        "attn_g": row(p["attn_norm_g"][li]),
        "mla_in": mla_in.astype(BF16),
        "q_g": row(p["mla_q_norm_g"][li]),
        "uq2": uq2.astype(BF16),
        "kv_g": row(p["mla_kv_norm_g"][li]),
        "ukv2": ukv2.astype(BF16),
        "krope_sel": krope_sel.astype(BF16),
        "fox_in": fox_in.astype(BF16),
        "fox_bias": jnp.pad(row(p["fox_f_bias"][li]), ((0, 0), (0, LANES - FOX_HEADS))),
        "fox_sel": fox_sel.astype(BF16),
        "diff_in": jnp.concatenate([w_dq, w_dk, w_dv], axis=1).astype(BF16),
        "gate_in": w_gl.astype(BF16),
        "diff_lambda": p["diff_lambda"][li],
        "diff_g": row(p["diff_subln_g"][li]),
        "branch": p["w_branch"][li].astype(BF16),
        "o": p["w_o"][li].astype(BF16),
        "ffn_g": row(p["ffn_norm_g"][li]),
        "ffn_in": p["w_ffn_in"][li].astype(BF16),
        "ffn_out": p["w_ffn_out"][li].astype(BF16),
    }


def _rope_tables(pos):
    half = MLA_ROPE // 2
    inv = 1.0 / (ROPE_BASE ** (jnp.arange(half, dtype=F32) / half))
    ang = pos.astype(F32)[:, None] * inv[None, :]
    cos, sin = jnp.cos(ang), jnp.sin(ang)
    cos2, sin2 = jnp.concatenate([cos, cos], axis=1), jnp.concatenate([sin, sin], axis=1)
    n = pos.shape[0]
    tail = jnp.zeros((n, LANES - MLA_NOPE - MLA_ROPE), F32)
    scale = (MLA_NOPE + MLA_ROPE) ** -0.5 * LOG2E
    cos_q = jnp.concatenate([jnp.ones((n, MLA_NOPE), F32), cos2, tail], axis=1) * scale
    sin_q = jnp.concatenate([jnp.zeros((n, MLA_NOPE), F32), sin2, tail], axis=1) * scale
    cos_k = jnp.concatenate([jnp.zeros((n, MLA_NOPE), F32), cos2, tail], axis=1)
    sin_k = jnp.concatenate([jnp.zeros((n, MLA_NOPE), F32), sin2, tail], axis=1)
    return cos_q, sin_q, cos_k, sin_k


def _rel_bucket(rel):
    half = REL_BUCKETS // 2
    max_exact = half // 2
    ret = jnp.where(rel > 0, half, 0)
    n = jnp.abs(rel)
    large = max_exact + (jnp.log(jnp.maximum(n, 1).astype(F32) / max_exact)
                         / math.log(REL_MAX_DIST / max_exact) * (half - max_exact)).astype(jnp.int32)
    large = jnp.minimum(large, half - 1)
    return ret + jnp.where(n < max_exact, n, large)


def _bias_tiles(rel_table, tq, tk, deltas, key_major):
    table = rel_table.astype(F32) * LOG2E
    rows, cols = (tk, tq) if key_major else (tq, tk)
    period = rows + cols
    m = jnp.arange(period, dtype=jnp.int32)
    m = jnp.where(m < cols, m, m - period)
    tiles = []
    for d in deltas:
        rel = (-m - d) if key_major else (m - d)
        bucket = _rel_bucket(rel)[None]
        line = jnp.zeros((DIFF_HEADS, period), F32)
        for b in range(REL_BUCKETS):
            line = jnp.where(bucket == b, table[b][:, None], line)
        flat = jnp.tile(line, (1, rows))[:, :rows * (period - 1)]
        tiles.append(flat.reshape(DIFF_HEADS, rows, period - 1)[:, :, :cols])
    return jnp.stack(tiles)


def _bias_plan(n, n_keys, tq, tk, q_off):
    nq, nk = n // tq, n_keys // tk
    step = math.gcd(tq, tk) if nq > 1 else tk
    needed = [q_off + qi * tq - kj * tk for qi in range(nq) for kj in range(nk)
              if ((kj * tk) >> CHUNK_SHIFT) <= ((q_off + qi * tq + tq - 1) >> CHUNK_SHIFT)]
    d_min = min(needed)
    far_from = tk + REL_MAX_DIST - 1
    deltas = list(range(d_min, max(far_from, d_min + 1), step))
    n_near = len(deltas)
    deltas.append(deltas[-1] + step)
    base = (q_off - d_min) // step

    def index(qi, kj):
        return jnp.clip(base + qi * (tq // step) - kj * (tk // step), 0, n_near)

    return deltas, index


def _run_trunk(x, past, weights, rel_table, final_g):
    B, n, D = x.shape
    depth = len(weights)
    n_past = past["mla_latent"].shape[2] if past else 0
    n_keys = n_past + n
    key_major = not past
    if past:
        tq, tk = n, _row_tile(n_past, SAMPLE_KEY_TILE)
        deltas, bias_index = _bias_plan(n, n_past, tq, tk, n_past)
        bias_new = _bias_tiles(rel_table, n, n, [0], False)
        past_rows = {name: past[name].reshape(depth, B, n_past, -1) for name in ("fox_k", "fox_v", "diff_k", "diff_v")}
    else:
        tq, tk = _row_tile(n, PROMPT_Q_TILE), _row_tile(n, PROMPT_K_TILE)
        deltas, bias_index = _bias_plan(n, n, tq, tk, 0)
    bias = _bias_tiles(rel_table, tq, tk, deltas, key_major)
    tm = _row_tile(n, ROW_TILE)
    pos = n_past + jnp.arange(n, dtype=jnp.int32)
    tabs = _rope_tables(pos)
    zero_carry = jnp.zeros((B, 1, LANES), F32)

    fox_shape = (depth, B, FOX_HEADS, FOX_DIM, n) if key_major else (depth, B, n, FOX_HEADS, FOX_DIM)
    rows = {
        "mla_latent": jnp.zeros((depth, B, n, MLA_KV_RANK), F32),
        "mla_krope": jnp.zeros((depth, B, n, MLA_ROPE), F32),
        "fox_k": jnp.zeros(fox_shape, F32),
        "fox_v": jnp.zeros(fox_shape, F32),
        "fox_logf": jnp.zeros((depth, B, n, FOX_HEADS), F32),
        "diff_k": jnp.zeros((depth, B, n, DIFF_HEADS, 2 * DIFF_QK), F32),
        "diff_v": jnp.zeros((depth, B, n, DIFF_HEADS, DIFF_V), F32),
    }
    y = None
    for li in range(depth):
        w = weights[li]
        q_a, k_a, v_a, rows["mla_latent"], rows["mla_krope"] = _mla_in(
            x, w, tabs, (rows["mla_latent"], rows["mla_krope"]), li, tm, key_major)
        carry = zero_carry
        if past:
            tp = _row_tile(n_past, PAST_ROW_TILE)
            k_a_p, v_a_p = _mla_past(past["mla_latent"], past["mla_krope"], li, w, tp)
            fk_p, carry = _fox_past(past_rows["fox_k"], past["fox_logf"], li, w, tp)
        q_b, k_b, v_b, _, rows["fox_k"], rows["fox_v"], rows["fox_logf"] = _fox_in(
            x, w, carry, (rows["fox_k"], rows["fox_v"], rows["fox_logf"]), li, tm, key_major)
        q_c, k_c, v_c, gates, rows["diff_k"], rows["diff_v"] = _diff_gate_in(
            x, w, (rows["diff_k"], rows["diff_v"]), li, tm, key_major)
        lam_init = 0.8 - 0.6 * math.exp(-0.3 * li)
        if past:
            o_a = _flash(q_a, k_a_p, v_a_p, k_a, v_a, li=li, heads=MLA_HEADS, dv=MLA_V, tk=tk, frame=False,
                         name="mla_flash")
            o_b = _flash(q_b, fk_p, past_rows["fox_v"], k_b, v_b, li=li, heads=FOX_HEADS, dv=FOX_DIM, tk=tk,
                         frame=True, name="fox_flash")
            o_c = _diff_flash(q_c, past_rows["diff_k"], past_rows["diff_v"], k_c, v_c, bias, bias_new,
                              w["diff_lambda"], w["diff_g"], li=li, tk=tk, lam_init=lam_init, bias_index=bias_index)
        else:
            o_a = _flash_t(q_a, k_a, v_a, heads=MLA_HEADS, dv=MLA_V, tq=tq, tk=tk, frame=False, name="mla_flash_t")
            o_b = _flash_t(q_b, k_b, v_b, heads=FOX_HEADS, dv=FOX_DIM, tq=tq, tk=tk, frame=True, name="fox_flash_t")
            o_c = _diff_flash_t(q_c, k_c, v_c, bias, w["diff_lambda"], w["diff_g"], tq=tq, tk=tk,
                                lam_init=lam_init, bias_index=bias_index)
        M = B * n
        flat = lambda a: a.reshape(M, a.shape[-1])
        x1 = _mix_out(flat(x), flat(o_a), flat(o_b), flat(o_c), flat(gates), w, _row_tile(M, MIX_ROW_TILE))
        x2, y = _ffn(x1, w, final_g if li == depth - 1 else None, _row_tile(M, ROW_TILE))
        x = x2.reshape(B, n, D)
    if key_major:
        for name in ("fox_k", "fox_v"):
            rows[name] = jnp.transpose(rows[name], (0, 1, 4, 2, 3))
    return y.reshape(B, n, D), rows


def kernel(x_prompt, x_sample, cache_mla_latent, cache_mla_krope, cache_fox_k, cache_fox_v, cache_fox_logf,
           cache_diff_k, cache_diff_v, attn_norm_g, w_in, mla_q_norm_g, mla_kv_norm_g, w_mla_uq, w_mla_ukv,
           fox_f_bias, diff_lambda, diff_subln_g, w_branch, w_o, ffn_norm_g, w_ffn_in, w_ffn_out,
           rel_bias_table, final_norm_g):
    params = {
        "attn_norm_g": attn_norm_g, "w_in": w_in, "mla_q_norm_g": mla_q_norm_g,
        "mla_kv_norm_g": mla_kv_norm_g, "w_mla_uq": w_mla_uq, "w_mla_ukv": w_mla_ukv,
        "fox_f_bias": fox_f_bias, "diff_lambda": diff_lambda, "diff_subln_g": diff_subln_g,
        "w_branch": w_branch, "w_o": w_o, "ffn_norm_g": ffn_norm_g, "w_ffn_in": w_ffn_in,
        "w_ffn_out": w_ffn_out,
    }
    past = {
        "mla_latent": cache_mla_latent, "mla_krope": cache_mla_krope, "fox_k": cache_fox_k,
        "fox_v": cache_fox_v, "fox_logf": cache_fox_logf, "diff_k": cache_diff_k, "diff_v": cache_diff_v,
    }
    depth = w_in.shape[0]
    weights = [_prep_layer_weights(li, params) for li in range(depth)]
    final_g = final_norm_g.reshape(1, -1)
    y_p, sp = _run_trunk(x_prompt, {}, weights, rel_bias_table, final_g)
    y_s, ss = _run_trunk(x_sample, past, weights, rel_bias_table, final_g)
    return (y_p, y_s,
            sp["mla_latent"], ss["mla_latent"], sp["mla_krope"], ss["mla_krope"],
            sp["fox_k"], ss["fox_k"], sp["fox_v"], ss["fox_v"],
            sp["fox_logf"], ss["fox_logf"], sp["diff_k"], ss["diff_k"],
            sp["diff_v"], ss["diff_v"])
```
